```python
import jax, jax.numpy as jnp
from jax import lax
import numpy as np

D_MODEL = 4096
BATCH = 4
SEQ = 2048
DEPTH = 2
DEC_BATCH = 128
DEC_SEQ = 1
PAST_LEN = 16384
PAGE_SIZE = 128

N_MIXERS = 2
N_SGU_LAYERS = (DEPTH + 1) // 2
N_CONV_LAYERS = DEPTH // 2
CHUNK = 128
SGU_WIDTH = D_MODEL
SGU_GROUPS = SGU_WIDTH // 128
CONV_WIDTH = 31
CONV_DIM = D_MODEL
N_EXPERT_GROUPS = 8
EXPERTS_PER_GROUP = 8
N_EXPERTS = N_EXPERT_GROUPS * EXPERTS_PER_GROUP
TOP_K_INNER = 2
D_EXPERT = D_MODEL // 4
MOE_BLOCK = 128
EPS = 1e-6

kernel_name = 'hybrid_sgu_conformer_hmoe_step'


def rms_norm(x, g):
    xf = x.astype(jnp.float32)
    y = xf * lax.rsqrt(jnp.mean(xf * xf, axis=-1, keepdims=True) + EPS)
    return (y * g.astype(jnp.float32)).astype(x.dtype)


def layer_norm(x, g, b):
    xf = x.astype(jnp.float32)
    xc = xf - jnp.mean(xf, axis=-1, keepdims=True)
    y = xc * lax.rsqrt(jnp.mean(xc * xc, axis=-1, keepdims=True) + EPS)
    return (y * g.astype(jnp.float32) + b.astype(jnp.float32)).astype(x.dtype)


def chunk_spatial_mix(v, w_s, b_s):
    bsz, length, width = v.shape
    n_chunks = -(-length // CHUNK)
    pad = n_chunks * CHUNK - length
    vc = jnp.pad(v, ((0, 0), (0, pad), (0, 0))).reshape(
        bsz, n_chunks, CHUNK, SGU_GROUPS, width // SGU_GROUPS)
    w = w_s * jnp.tril(jnp.ones((CHUNK, CHUNK), w_s.dtype))
    s = jnp.einsum('gts,bnsgc->bntgc', w, vc) + b_s.T[None, None, :, :, None]
    return s.reshape(bsz, n_chunks * CHUNK, width)[:, :length]


def sgu_mixer(xn, w_in, b_in, v_g, v_b, w_s, b_s, w_out, b_out):
    z = jax.nn.gelu(xn @ w_in + b_in)
    u, v = jnp.split(z, 2, axis=-1)
    v = layer_norm(v, v_g, v_b)
    s = chunk_spatial_mix(v, w_s, b_s)
    return (u * s) @ w_out + b_out, v


def conv_module(xn, hist, w_pw1, b_pw1, w_dw, b_dw, ln_g, ln_b, w_pw2, b_pw2):
    a, gate = jnp.split(xn @ w_pw1 + b_pw1, 2, axis=-1)
    glu = a * jax.nn.sigmoid(gate)
    if hist is None:
        hist = jnp.zeros((glu.shape[0], CONV_WIDTH - 1, CONV_DIM), glu.dtype)
    full = jnp.concatenate([hist.astype(glu.dtype), glu], axis=1)
    y = lax.conv_general_dilated(
        full, w_dw[:, None, :].astype(glu.dtype), window_strides=(1,), padding='VALID',
        dimension_numbers=('NWC', 'WIO', 'NWC'), feature_group_count=CONV_DIM) + b_dw
    y = jax.nn.silu(layer_norm(y, ln_g, ln_b))
    return y @ w_pw2 + b_pw2, full[:, -(CONV_WIDTH - 1):]


def routed_experts(xf, eid, gate, w_gate, w_up, w_down):
    n_tok, d = xf.shape
    n_assign = n_tok * TOP_K_INNER
    e_flat = eid.reshape(n_assign)
    order = jnp.argsort(e_flat)
    e_sorted = e_flat[order]
    tok_sorted = (order // TOP_K_INNER).astype(jnp.int32)
    g_sorted = gate.reshape(n_assign)[order]
    counts = jnp.bincount(e_flat, length=N_EXPERTS)
    padded = (counts + MOE_BLOCK - 1) // MOE_BLOCK * MOE_BLOCK
    start = jnp.cumsum(counts) - counts
    pend = jnp.cumsum(padded)
    pstart = pend - padded
    dest = pstart[e_sorted] + jnp.arange(n_assign) - start[e_sorted]
    n_blocks = -(-(n_assign + N_EXPERTS * (MOE_BLOCK - 1)) // MOE_BLOCK)
    n_rows = n_blocks * MOE_BLOCK
    row_tok = jnp.full((n_rows,), n_tok, jnp.int32).at[dest].set(tok_sorted)
    row_gate = jnp.zeros((n_rows,), jnp.float32).at[dest].set(g_sorted)
    blk_start = jnp.arange(n_blocks) * MOE_BLOCK
    blk_expert = jnp.minimum(jnp.searchsorted(pend, blk_start, side='right'), N_EXPERTS - 1)
    blk_used = blk_start < pend[-1]
    x_pad = jnp.concatenate([xf, jnp.zeros((1, d), xf.dtype)], axis=0)
    xb = x_pad[row_tok].reshape(n_blocks, MOE_BLOCK, d)

    def block_ffn(args):
        xblk, e, used = args
        def run(xx):
            h = jax.nn.silu(xx @ w_gate[e]) * (xx @ w_up[e])
            return h @ w_down[e]
        return lax.cond(used, run, jnp.zeros_like, xblk)

    yb = lax.map(block_ffn, (xb, blk_expert, blk_used))
    contrib = (yb.reshape(n_rows, d).astype(jnp.float32) * row_gate[:, None]).astype(xf.dtype)
    return jnp.zeros((n_tok + 1, d), xf.dtype).at[row_tok].add(contrib)[:n_tok]


def hier_moe(xf, w_group, b_group, w_expert, b_expert, w_gate, w_up, w_down):
    n_tok = xf.shape[0]
    g_logits = (xf @ w_group + b_group).astype(jnp.float32)
    g_prob = jax.nn.softmax(g_logits, axis=-1)
    grp = jnp.argmax(g_logits, axis=-1).astype(jnp.int32)
    p_grp = jnp.take_along_axis(g_prob, grp[:, None], axis=1)
    e_logits = (xf @ w_expert + b_expert).astype(jnp.float32).reshape(
        n_tok, N_EXPERT_GROUPS, EXPERTS_PER_GROUP)
    in_grp = jnp.take_along_axis(e_logits, grp[:, None, None], axis=1)[:, 0]
    top_v, top_j = lax.top_k(in_grp, TOP_K_INNER)
    gate = p_grp * jax.nn.softmax(top_v, axis=-1)
    eid = grp[:, None] * EXPERTS_PER_GROUP + top_j.astype(jnp.int32)
    return routed_experts(xf, eid, gate, w_gate, w_up, w_down)


def setup_inputs(seed: int = 0) -> dict:
    key = jax.random.key(seed)
    keys = iter(jax.random.split(key, 40))

    def nrm(shape, scale):
        return jax.random.normal(next(keys), shape, jnp.float32) * scale

    na, nb, nl = N_SGU_LAYERS, N_CONV_LAYERS, DEPTH
    return {
        'x_prompt': nrm((BATCH, SEQ, D_MODEL), 1.0),
        'x_sample': nrm((DEC_BATCH, DEC_SEQ, D_MODEL), 1.0),
        'state_conv': nrm((nb, DEC_BATCH, CONV_WIDTH - 1, CONV_DIM), 0.5),
        'mix_norm_g': 1.0 + nrm((nl, D_MODEL), 0.02),
        'sgu_w_in': nrm((na, D_MODEL, 2 * SGU_WIDTH), D_MODEL ** -0.5),
        'sgu_b_in': nrm((na, 2 * SGU_WIDTH), 0.02),
        'sgu_v_norm_g': 1.0 + nrm((na, SGU_WIDTH), 0.02),
        'sgu_v_norm_b': nrm((na, SGU_WIDTH), 0.02),
        'sgu_w_s': nrm((na, SGU_GROUPS, CHUNK, CHUNK), 0.5 * CHUNK ** -0.5),
        'sgu_b_s': 1.0 + nrm((na, SGU_GROUPS, CHUNK), 0.02),
        'sgu_w_out': nrm((na, SGU_WIDTH, D_MODEL), SGU_WIDTH ** -0.5),
        'sgu_b_out': nrm((na, D_MODEL), 0.02),
        'conv_w_pw1': nrm((nb, D_MODEL, 2 * CONV_DIM), D_MODEL ** -0.5),
        'conv_b_pw1': nrm((nb, 2 * CONV_DIM), 0.02),
        'conv_w_dw': nrm((nb, CONV_WIDTH, CONV_DIM), CONV_WIDTH ** -0.5),
        'conv_b_dw': nrm((nb, CONV_DIM), 0.02),
        'conv_ln_g': 1.0 + nrm((nb, CONV_DIM), 0.02),
        'conv_ln_b': nrm((nb, CONV_DIM), 0.02),
        'conv_w_pw2': nrm((nb, CONV_DIM, D_MODEL), CONV_DIM ** -0.5),
        'conv_b_pw2': nrm((nb, D_MODEL), 0.02),
        'ffn_norm_g': 1.0 + nrm((nl, D_MODEL), 0.02),
        'moe_w_group': nrm((nl, D_MODEL, N_EXPERT_GROUPS), D_MODEL ** -0.5),
        'moe_b_group': nrm((nl, N_EXPERT_GROUPS), 0.01),
        'moe_w_expert': nrm((nl, D_MODEL, N_EXPERTS), D_MODEL ** -0.5),
        'moe_b_expert': nrm((nl, N_EXPERTS), 0.01),
        'moe_w_gate': nrm((nl, N_EXPERTS, D_MODEL, D_EXPERT), D_MODEL ** -0.5),
        'moe_w_up': nrm((nl, N_EXPERTS, D_MODEL, D_EXPERT), D_MODEL ** -0.5),
        'moe_w_down': nrm((nl, N_EXPERTS, D_EXPERT, D_MODEL), D_EXPERT ** -0.5),
        'final_norm_g': 1.0 + nrm((D_MODEL,), 0.02),
    }


def reference(x_prompt, x_sample, state_conv, mix_norm_g,
              sgu_w_in, sgu_b_in, sgu_v_norm_g, sgu_v_norm_b, sgu_w_s, sgu_b_s, sgu_w_out, sgu_b_out,
              conv_w_pw1, conv_b_pw1, conv_w_dw, conv_b_dw, conv_ln_g, conv_ln_b, conv_w_pw2, conv_b_pw2,
              ffn_norm_g, moe_w_group, moe_b_group, moe_w_expert, moe_b_expert,
              moe_w_gate, moe_w_up, moe_w_down, final_norm_g):
    h_p, h_s = x_prompt, x_sample
    sgu_v_new, conv_p_new, conv_s_new = [], [], []
    for i in range(DEPTH):
        j = i // N_MIXERS
        hn_p = rms_norm(h_p, mix_norm_g[i])
        hn_s = rms_norm(h_s, mix_norm_g[i])
        if i % N_MIXERS == 0:
            prm = (sgu_w_in[j], sgu_b_in[j], sgu_v_norm_g[j], sgu_v_norm_b[j],
                   sgu_w_s[j], sgu_b_s[j], sgu_w_out[j], sgu_b_out[j])
            m_p, _ = sgu_mixer(hn_p, *prm)
            m_s, v_s = sgu_mixer(hn_s, *prm)
            sgu_v_new.append(v_s)
        else:
            prm = (conv_w_pw1[j], conv_b_pw1[j], conv_w_dw[j], conv_b_dw[j],
                   conv_ln_g[j], conv_ln_b[j], conv_w_pw2[j], conv_b_pw2[j])
            m_p, c_p = conv_module(hn_p, None, *prm)
            m_s, c_s = conv_module(hn_s, state_conv[j], *prm)
            conv_p_new.append(c_p)
            conv_s_new.append(c_s)
        h_p = h_p + m_p
        h_s = h_s + m_s
        n_p = h_p.shape[0] * h_p.shape[1]
        tok = jnp.concatenate([h_p.reshape(n_p, D_MODEL),
                               h_s.reshape(-1, D_MODEL)], axis=0)
        f = hier_moe(rms_norm(tok, ffn_norm_g[i]), moe_w_group[i], moe_b_group[i],
                     moe_w_expert[i], moe_b_expert[i], moe_w_gate[i], moe_w_up[i], moe_w_down[i])
        h_p = h_p + f[:n_p].reshape(h_p.shape)
        h_s = h_s + f[n_p:].reshape(h_s.shape)
    y_prompt = rms_norm(h_p, final_norm_g)
    y_sample = rms_norm(h_s, final_norm_g)
    state_sgu_v_sample = jnp.stack(sgu_v_new)
    state_conv_prompt = jnp.stack(conv_p_new)
    state_conv_sample = jnp.stack(conv_s_new)
    return (y_prompt, y_sample, state_sgu_v_sample, state_conv_prompt, state_conv_sample)
```

```python
import functools

import jax
import jax.numpy as jnp
from jax import lax
from jax.experimental import pallas as pl
from jax.experimental.pallas import tpu as pltpu

F32 = jnp.float32
BF16 = jnp.bfloat16
I32 = jnp.int32

EPS = 1e-6
CHUNK = 128
LANES = 128
SUBLANES = 8
CONV_WIDTH = 31
N_EXPERT_GROUPS = 8
EXPERTS_PER_GROUP = 8
N_EXPERTS = N_EXPERT_GROUPS * EXPERTS_PER_GROUP
TOP_K = 2
MOE_BLOCK = 128
PASS_BLOCKS = 4
EXPERT_LANE0 = N_EXPERT_GROUPS
VMEM_LIMIT_BYTES = 56 * 1024 * 1024


def _params(*sem):
    return pltpu.CompilerParams(dimension_semantics=sem, vmem_limit_bytes=VMEM_LIMIT_BYTES)


def _largest_tile(n, cap, mult):
    t = (min(cap, n) // mult) * mult
    while t > mult and n % t:
        t -= mult
    assert t >= mult and n % t == 0, (n, cap, mult)
    return t


def _cast_kernel(x_ref, o_ref):
    o_ref[...] = x_ref[...].astype(o_ref.dtype)


def _cast_bf16(w):
    k, n = w.shape
    tk = _largest_tile(k, 256, 16)
    return pl.pallas_call(
        _cast_kernel,
        out_shape=jax.ShapeDtypeStruct((k, n), BF16),
        grid=(k // tk,),
        in_specs=[pl.BlockSpec((tk, n), lambda i: (i, 0))],
        out_specs=pl.BlockSpec((tk, n), lambda i: (i, 0)),
        compiler_params=_params("arbitrary"),
        name="cast_bf16",
    )(w)


NORM_TILE = 320


def _rmsnorm_kernel(x_ref, g_ref, o_ref):
    x = x_ref[...]
    y = x * lax.rsqrt(jnp.mean(x * x, axis=-1, keepdims=True) + EPS)
    o_ref[...] = (y * g_ref[...]).astype(o_ref.dtype)


def _rmsnorm(x, g, out_dtype, row0=0, nrows=None):
    t, d = x.shape
    nrows = t if nrows is None else nrows
    tm = _largest_tile(nrows, NORM_TILE, 16)
    assert row0 % tm == 0
    b0 = row0 // tm
    return pl.pallas_call(
        _rmsnorm_kernel,
        out_shape=jax.ShapeDtypeStruct((nrows, d), out_dtype),
        grid=(nrows // tm,),
        in_specs=[pl.BlockSpec((tm, d), lambda i: (i + b0, 0)),
                  pl.BlockSpec((1, d), lambda i: (0, 0))],
        out_specs=pl.BlockSpec((tm, d), lambda i: (i, 0)),
        compiler_params=_params("arbitrary"),
        name="rmsnorm",
    )(x, g.reshape(1, d))


def _sigmoid(x):
    return 1.0 / (1.0 + jnp.exp(-x))


def _mm_gelu_kernel(x_ref, w_ref, b_ref, o_ref):
    acc = jnp.dot(x_ref[...], w_ref[...], preferred_element_type=F32) + b_ref[...]
    o_ref[...] = jax.nn.gelu(acc).astype(o_ref.dtype)


def _mm_glu_kernel(x_ref, wa_ref, wg_ref, ba_ref, bg_ref, o_ref):
    x = x_ref[...]
    a = jnp.dot(x, wa_ref[...], preferred_element_type=F32) + ba_ref[...]
    gate = jnp.dot(x, wg_ref[...], preferred_element_type=F32) + bg_ref[...]
    o_ref[...] = (a * _sigmoid(gate)).astype(o_ref.dtype)


def _mm_res_kernel(x_ref, w_ref, b_ref, r_ref, o_ref):
    acc = jnp.dot(x_ref[...], w_ref[...], preferred_element_type=F32) + b_ref[...]
    o_ref[...] = (r_ref[...] + acc).astype(o_ref.dtype)


def _mm_tiles(t, n):
    return _largest_tile(t, 832, 16), _largest_tile(n, 1024, LANES)


def _mm_gelu(x, w, b):
    t, k = x.shape
    n = w.shape[1]
    tm, tn = _mm_tiles(t, n)
    return pl.pallas_call(
        _mm_gelu_kernel,
        out_shape=jax.ShapeDtypeStruct((t, n), F32),
        grid=(t // tm, n // tn),
        in_specs=[pl.BlockSpec((tm, k), lambda i, j: (i, 0)),
                  pl.BlockSpec((k, tn), lambda i, j: (0, j)),
                  pl.BlockSpec((1, tn), lambda i, j: (0, j))],
        out_specs=pl.BlockSpec((tm, tn), lambda i, j: (i, j)),
        compiler_params=_params("arbitrary", "arbitrary"),
        name="mm_gelu",
    )(x, w, b.reshape(1, n))


def _mm_glu(x, w, b):
    t, k = x.shape
    n = w.shape[1] // 2
    tm, tn = _mm_tiles(t, n)
    tn = min(tn, 512)
    nb = n // tn
    b2 = b.reshape(1, 2 * n)
    return pl.pallas_call(
        _mm_glu_kernel,
        out_shape=jax.ShapeDtypeStruct((t, n), F32),
        grid=(t // tm, nb),
        in_specs=[pl.BlockSpec((tm, k), lambda i, j: (i, 0)),
                  pl.BlockSpec((k, tn), lambda i, j: (0, j)),
                  pl.BlockSpec((k, tn), lambda i, j: (0, j + nb)),
                  pl.BlockSpec((1, tn), lambda i, j: (0, j)),
                  pl.BlockSpec((1, tn), lambda i, j: (0, j + nb))],
        out_specs=pl.BlockSpec((tm, tn), lambda i, j: (i, j)),
        compiler_params=_params("arbitrary", "arbitrary"),
        name="mm_glu",
    )(x, w, w, b2, b2)


def _mm_res(x, w, b, res):
    t, k = x.shape
    n = w.shape[1]
    tm, tn = _mm_tiles(t, n)
    return pl.pallas_call(
        _mm_res_kernel,
        out_shape=jax.ShapeDtypeStruct((t, n), F32),
        grid=(t // tm, n // tn),
        in_specs=[pl.BlockSpec((tm, k), lambda i, j: (i, 0)),
                  pl.BlockSpec((k, tn), lambda i, j: (0, j)),
                  pl.BlockSpec((1, tn), lambda i, j: (0, j)),
                  pl.BlockSpec((tm, tn), lambda i, j: (i, j))],
        out_specs=pl.BlockSpec((tm, tn), lambda i, j: (i, j)),
        compiler_params=_params("arbitrary", "arbitrary"),
        name="mm_res",
    )(x, w, b.reshape(1, n), res)


def _layer_norm(x, g, b):
    xc = x - jnp.mean(x, axis=-1, keepdims=True)
    y = xc * lax.rsqrt(jnp.mean(xc * xc, axis=-1, keepdims=True) + EPS)
    return y * g + b


def _sgu_mid_kernel(u_ref, v_ref, g_ref, b_ref, ws_ref, bias_ref, w00_ref, b00_ref,
                    us_ref, vs_ref, wt_ref, *, n_groups, n_prompt_tiles):
    i = pl.program_id(0)

    @pl.when(i == 0)
    def _():
        r = lax.broadcasted_iota(I32, (CHUNK, CHUNK), 0)
        c = lax.broadcasted_iota(I32, (CHUNK, CHUNK), 1)
        tril = (c <= r).astype(F32)
        for g in range(n_groups):
            wt_ref[g] = (ws_ref[g] * tril).astype(BF16)

    vn = _layer_norm(v_ref[...], g_ref[...], b_ref[...])

    @pl.when(i < n_prompt_tiles)
    def _():
        vb = vn.astype(BF16)
        for g in range(n_groups):
            sl = slice(g * LANES, (g + 1) * LANES)
            s = jnp.dot(wt_ref[g], vb[:, sl], preferred_element_type=F32) + bias_ref[:, sl]
            us_ref[:, sl] = (u_ref[:, sl] * s).astype(us_ref.dtype)

    @pl.when(i >= n_prompt_tiles)
    def _():
        s = vn * w00_ref[...] + b00_ref[...]
        us_ref[...] = (u_ref[...] * s).astype(us_ref.dtype)
        vs_ref[...] = vn


def _sgu_mid(z, v_g, v_b, w_s, b_s, n_prompt_rows):
    t, w2 = z.shape
    w = w2 // 2
    n_groups = w // LANES
    assert w_s.shape == (n_groups, CHUNK, CHUNK) and t % CHUNK == 0 and n_prompt_rows % CHUNK == 0
    npt = n_prompt_rows // CHUNK
    nst = t // CHUNK - npt
    bias_full = jnp.repeat(b_s.T, LANES, axis=1)
    w00 = jnp.repeat(w_s[:, 0, 0], LANES).reshape(1, w)
    b00 = jnp.repeat(b_s[:, 0], LANES).reshape(1, w)
    kern = functools.partial(_sgu_mid_kernel, n_groups=n_groups, n_prompt_tiles=npt)
    return pl.pallas_call(
        kern,
        out_shape=(jax.ShapeDtypeStruct((t, w), BF16),
                   jax.ShapeDtypeStruct((nst * CHUNK, w), F32)),
        grid=(t // CHUNK,),
        in_specs=[pl.BlockSpec((CHUNK, w), lambda i: (i, 0)),
                  pl.BlockSpec((CHUNK, w), lambda i: (i, 1)),
                  pl.BlockSpec((1, w), lambda i: (0, 0)),
                  pl.BlockSpec((1, w), lambda i: (0, 0)),
                  pl.BlockSpec((n_groups, CHUNK, CHUNK), lambda i: (0, 0, 0)),
                  pl.BlockSpec((CHUNK, w), lambda i: (0, 0)),
                  pl.BlockSpec((1, w), lambda i: (0, 0)),
                  pl.BlockSpec((1, w), lambda i: (0, 0))],
        out_specs=(pl.BlockSpec((CHUNK, w), lambda i: (i, 0)),
                   pl.BlockSpec((CHUNK, w), lambda i: (jnp.maximum(i - npt, 0), 0))),
        scratch_shapes=[pltpu.VMEM((n_groups, CHUNK, CHUNK), BF16)],
        compiler_params=_params("arbitrary"),
        name="sgu_mid",
    )(z, z, v_g.reshape(1, w), v_b.reshape(1, w), w_s, bias_full, w00, b00)


CONV_HALO = 32
CONV_TILE = 128


def _conv_prompt_kernel(x_ref, halo_ref, ysamp_ref, w_ref, bdw_ref, g_ref, b_ref, o_ref, ext_ref, sh_ref,
                        *, tiles_per_seq, n_prompt_tiles):
    i = pl.program_id(0)
    ext_rows = CONV_HALO + CONV_TILE

    @pl.when(i >= n_prompt_tiles)
    def _():
        o_ref[...] = ysamp_ref[...]

    @pl.when(i < n_prompt_tiles)
    def _():
        @pl.when(i % tiles_per_seq == 0)
        def _():
            ext_ref[0:CONV_HALO, :] = jnp.zeros((CONV_HALO, ext_ref.shape[1]), F32)

        @pl.when(i % tiles_per_seq != 0)
        def _():
            ext_ref[0:CONV_HALO, :] = halo_ref[...]

        ext_ref[CONV_HALO:ext_rows, :] = x_ref[...]
        sh_rows = sh_ref.shape[1]
        for r in range(1, SUBLANES):
            sh_ref[r - 1] = ext_ref[r:r + sh_rows, :]

        first = CONV_HALO - (CONV_WIDTH - 1)

        def body(rb, carry):
            acc = jnp.broadcast_to(bdw_ref[...], (SUBLANES, o_ref.shape[1]))
            for k in range(CONV_WIDTH):
                m, r = divmod(first + k, SUBLANES)
                row = pl.ds(pl.multiple_of(rb * SUBLANES + m * SUBLANES, SUBLANES), SUBLANES)
                tap = ext_ref[row, :] if r == 0 else sh_ref[r - 1, row, :]
                acc = acc + tap * w_ref[k:k + 1, :]
            y = _layer_norm(acc, g_ref[...], b_ref[...])
            o_ref[pl.ds(pl.multiple_of(rb * SUBLANES, SUBLANES), SUBLANES), :] = (
                y * _sigmoid(y)).astype(o_ref.dtype)
            return carry

        lax.fori_loop(0, CONV_TILE // SUBLANES, body, 0)


def _conv_prompt(glu, y_sample, w_dw, b_dw, ln_g, ln_b, n_prompt_rows, seq):
    t, c = glu.shape
    assert seq % CONV_TILE == 0 and n_prompt_rows % seq == 0 and t % CONV_TILE == 0
    assert y_sample.shape == (t - n_prompt_rows, c)
    ratio = CONV_TILE // CONV_HALO
    npt = n_prompt_rows // CONV_TILE
    kern = functools.partial(_conv_prompt_kernel, tiles_per_seq=seq // CONV_TILE, n_prompt_tiles=npt)

    def prompt_tile(i):
        return jnp.minimum(i, npt - 1)

    return pl.pallas_call(
        kern,
        out_shape=jax.ShapeDtypeStruct((t, c), BF16),
        grid=(t // CONV_TILE,),
        in_specs=[pl.BlockSpec((CONV_TILE, c), lambda i: (prompt_tile(i), 0)),
                  pl.BlockSpec((CONV_HALO, c), lambda i: (jnp.maximum(prompt_tile(i) * ratio - 1, 0), 0)),
                  pl.BlockSpec((CONV_TILE, c), lambda i: (jnp.maximum(i - npt, 0), 0)),
                  pl.BlockSpec((CONV_WIDTH, c), lambda i: (0, 0)),
                  pl.BlockSpec((1, c), lambda i: (0, 0)),
                  pl.BlockSpec((1, c), lambda i: (0, 0)),
                  pl.BlockSpec((1, c), lambda i: (0, 0))],
        out_specs=pl.BlockSpec((CONV_TILE, c), lambda i: (i, 0)),
        scratch_shapes=[pltpu.VMEM((CONV_HALO + CONV_TILE, c), F32),
                        pltpu.VMEM((SUBLANES - 1, CONV_HALO + CONV_TILE - SUBLANES, c), F32)],
        compiler_params=_params("arbitrary"),
        name="conv_prompt",
    )(glu, glu, y_sample, w_dw, b_dw.reshape(1, c), ln_g.reshape(1, c), ln_b.reshape(1, c))


CONV_SAMPLE_TILE = 16


def _conv_sample_kernel(hist_ref, x_ref, w_ref, bdw_ref, g_ref, b_ref, yc_ref, ns_ref):
    nh = CONV_WIDTH - 1
    hist = hist_ref[...]
    x = x_ref[...]
    y = jnp.sum(hist * w_ref[0:nh, :][None], axis=1) + x * w_ref[nh:nh + 1, :] + bdw_ref[...]
    y = _layer_norm(y, g_ref[...], b_ref[...])
    yc_ref[...] = (y * _sigmoid(y)).astype(yc_ref.dtype)
    ns_ref[:, 0:nh - 1, :] = hist_ref[:, 1:nh, :]
    for b in range(CONV_SAMPLE_TILE):
        ns_ref[b, nh - 1:nh, :] = x_ref[b:b + 1, :]


def _conv_sample(hist, glu, w_dw, b_dw, ln_g, ln_b, n_prompt_rows):
    t, c = glu.shape
    db, nh, _ = hist.shape
    tb = CONV_SAMPLE_TILE
    assert nh == CONV_WIDTH - 1 and db % tb == 0 and n_prompt_rows % tb == 0 and t == n_prompt_rows + db
    b0 = n_prompt_rows // tb
    return pl.pallas_call(
        _conv_sample_kernel,
        out_shape=(jax.ShapeDtypeStruct((db, c), BF16),
                   jax.ShapeDtypeStruct((db, nh, c), F32)),
        grid=(db // tb,),
        in_specs=[pl.BlockSpec((tb, nh, c), lambda i: (i, 0, 0)),
                  pl.BlockSpec((tb, c), lambda i: (i + b0, 0)),
                  pl.BlockSpec((CONV_WIDTH, c), lambda i: (0, 0)),
                  pl.BlockSpec((1, c), lambda i: (0, 0)),
                  pl.BlockSpec((1, c), lambda i: (0, 0)),
                  pl.BlockSpec((1, c), lambda i: (0, 0))],
        out_specs=(pl.BlockSpec((tb, c), lambda i: (i, 0)),
                   pl.BlockSpec((tb, nh, c), lambda i: (i, 0, 0))),
        compiler_params=_params("arbitrary"),
        name="conv_sample",
    )(hist, glu, w_dw, b_dw.reshape(1, c), ln_g.reshape(1, c), ln_b.reshape(1, c))


ROUTER_TILE = 320


def _pick_lanes(lane, *cols):
    out = jnp.zeros((cols[0].shape[0], LANES), cols[0].dtype)
    for k, col in enumerate(cols):
        out = jnp.where(lane == k, col, out)
    return out


def _router_kernel(h_ref, g_ref, wr_ref, br_ref, xf_ref, meta_ref, gate_ref, cnt_ref, carry_ref):
    i = pl.program_id(0)
    tm = h_ref.shape[0]

    @pl.when(i == 0)
    def _():
        carry_ref[...] = jnp.zeros_like(carry_ref)

    x = h_ref[...]
    xf = x * lax.rsqrt(jnp.mean(x * x, axis=-1, keepdims=True) + EPS) * g_ref[...]
    xf_ref[...] = xf
    logits = jnp.dot(xf, wr_ref[...], preferred_element_type=F32,
                     precision=lax.Precision.HIGHEST) + br_ref[...]
    lane = lax.broadcasted_iota(I32, (tm, LANES), 1)
    lane_f = lane.astype(F32)
    neg_inf = jnp.float32(-jnp.inf)

    def first_lane(hit):
        return jnp.min(jnp.where(hit, lane_f, float(LANES)), axis=-1, keepdims=True).astype(I32)

    is_g = lane < N_EXPERT_GROUPS
    gl = jnp.where(is_g, logits, neg_inf)
    gmax = jnp.max(gl, axis=-1, keepdims=True)
    grp = first_lane(gl == gmax)
    p_grp = 1.0 / jnp.sum(jnp.where(is_g, jnp.exp(logits - gmax), 0.0), axis=-1, keepdims=True)

    lo = EXPERT_LANE0 + grp * EXPERTS_PER_GROUP
    in_grp = (lane >= lo) & (lane < lo + EXPERTS_PER_GROUP)
    el = jnp.where(in_grp, logits, neg_inf)
    v0 = jnp.max(el, axis=-1, keepdims=True)
    j0 = first_lane(el == v0)
    el1 = jnp.where(lane == j0, neg_inf, el)
    v1 = jnp.max(el1, axis=-1, keepdims=True)
    j1 = first_lane(el1 == v1)
    e1 = jnp.exp(v1 - v0)
    gate0 = p_grp * (1.0 / (1.0 + e1))
    gate1 = p_grp * (e1 / (1.0 + e1))

    oh0 = lane == j0
    oh1 = lane == j1
    cnt = jnp.where(oh0 | oh1, 1.0, 0.0)
    r = lax.broadcasted_iota(I32, (tm, tm), 0)
    c = lax.broadcasted_iota(I32, (tm, tm), 1)
    strict_lower = jnp.where(c < r, 1.0, 0.0).astype(BF16)
    before = jnp.dot(strict_lower, cnt.astype(BF16), preferred_element_type=F32) + carry_ref[...]
    rank0 = jnp.sum(jnp.where(oh0, before, 0.0), axis=-1, keepdims=True).astype(I32)
    rank1 = jnp.sum(jnp.where(oh1, before, 0.0), axis=-1, keepdims=True).astype(I32)
    carry_ref[...] = carry_ref[...] + jnp.sum(cnt, axis=0, keepdims=True)
    cnt_ref[...] = carry_ref[...]

    meta_ref[...] = _pick_lanes(lane, j0 - EXPERT_LANE0, j1 - EXPERT_LANE0, rank0, rank1)
    gate_ref[...] = _pick_lanes(lane, gate0, gate1)


def _router(h, g, w_group, b_group, w_expert, b_expert):
    t, d = h.shape
    tm = _largest_tile(t, ROUTER_TILE, SUBLANES)
    pad = LANES - N_EXPERT_GROUPS - N_EXPERTS
    wr = jnp.concatenate([w_group, w_expert, jnp.zeros((d, pad), F32)], axis=1)
    br = jnp.concatenate([b_group, b_expert, jnp.zeros((pad,), F32)]).reshape(1, LANES)
    return pl.pallas_call(
        _router_kernel,
        out_shape=(jax.ShapeDtypeStruct((t, d), F32),
                   jax.ShapeDtypeStruct((t, LANES), I32),
                   jax.ShapeDtypeStruct((t, LANES), F32),
                   jax.ShapeDtypeStruct((1, LANES), F32)),
        grid=(t // tm,),
        in_specs=[pl.BlockSpec((tm, d), lambda i: (i, 0)),
                  pl.BlockSpec((1, d), lambda i: (0, 0)),
                  pl.BlockSpec((d, LANES), lambda i: (0, 0)),
                  pl.BlockSpec((1, LANES), lambda i: (0, 0))],
        out_specs=(pl.BlockSpec((tm, d), lambda i: (i, 0)),
                   pl.BlockSpec((tm, LANES), lambda i: (i, 0)),
                   pl.BlockSpec((tm, LANES), lambda i: (i, 0)),
                   pl.BlockSpec((1, LANES), lambda i: (0, 0))),
        scratch_shapes=[pltpu.VMEM((1, LANES), F32)],
        compiler_params=_params("arbitrary"),
        name="moe_router",
    )(h, g.reshape(1, d), wr, br)


def _row_copy(src, src_row, dst, dst_row, sem):
    return pltpu.make_async_copy(src.at[pl.ds(src_row, 1)], dst.at[pl.ds(dst_row, 1)], sem)


def _block_rows(ref, blk):
    return ref.at[pl.ds(pl.multiple_of(blk * MOE_BLOCK, MOE_BLOCK), MOE_BLOCK)]


def _dispatch_kernel(zflag_ref, xf_ref, meta_ref, pstart_ref, xs_ref, dest_ref,
                     dest_vmem, dest_smem, zbuf, sem, row_sem, zero_sem):
    tm = xf_ref.shape[0]

    @pl.when(pl.program_id(0) == 0)
    def _():
        zbuf[...] = jnp.zeros_like(zbuf)

        def zero_block(blk, carry, wait):
            @pl.when(zflag_ref[blk] != 0)
            def _():
                cp = pltpu.make_async_copy(zbuf, _block_rows(xs_ref, blk), zero_sem)
                if wait:
                    cp.wait()
                else:
                    cp.start()
            return carry

        n_blocks = zflag_ref.shape[0]
        lax.fori_loop(0, n_blocks, functools.partial(zero_block, wait=False), 0)
        lax.fori_loop(0, n_blocks, functools.partial(zero_block, wait=True), 0)

    lane = lax.broadcasted_iota(I32, (tm, LANES), 1)
    meta = meta_ref[...]
    pstart = pstart_ref[...]
    dests = []
    for k in range(TOP_K):
        e = meta[:, k:k + 1]
        rank = meta[:, TOP_K + k:TOP_K + k + 1]
        base = jnp.sum(jnp.where(lane - EXPERT_LANE0 == e, pstart, 0.0), axis=-1, keepdims=True)
        dests.append(base + rank.astype(F32))
    dest_t = jnp.transpose(_pick_lanes(lane, *dests))[0:SUBLANES, :].astype(I32)
    dest_ref[0] = dest_t
    dest_vmem[...] = dest_t
    cp = pltpu.make_async_copy(dest_vmem, dest_smem, sem)
    cp.start()
    cp.wait()

    def issue(tok, carry):
        for k in range(TOP_K):
            _row_copy(xf_ref, tok, xs_ref, dest_smem[k, tok], row_sem).start()
        return carry

    lax.fori_loop(0, tm, issue, 0)

    def drain(tok, carry):
        for k in range(TOP_K):
            _row_copy(xf_ref, 0, xs_ref, 0, row_sem).wait()
        return carry

    lax.fori_loop(0, tm, drain, 0)


def _dispatch(xf, meta, pstart_lanes, zero_flags):
    t, d = xf.shape
    tm = CHUNK
    nt = t // tm
    n_rows = zero_flags.shape[0] * MOE_BLOCK
    grid_spec = pltpu.PrefetchScalarGridSpec(
        num_scalar_prefetch=1,
        grid=(nt,),
        in_specs=[pl.BlockSpec((tm, d), lambda i, zf: (i, 0)),
                  pl.BlockSpec((tm, LANES), lambda i, zf: (i, 0)),
                  pl.BlockSpec((1, LANES), lambda i, zf: (0, 0))],
        out_specs=(pl.BlockSpec(memory_space=pl.ANY),
                   pl.BlockSpec((1, SUBLANES, LANES), lambda i, zf: (i, 0, 0))),
        scratch_shapes=[pltpu.VMEM((SUBLANES, LANES), I32),
                        pltpu.SMEM((SUBLANES, LANES), I32),
                        pltpu.VMEM((MOE_BLOCK, d), F32),
                        pltpu.SemaphoreType.DMA,
                        pltpu.SemaphoreType.DMA,
                        pltpu.SemaphoreType.DMA])
    return pl.pallas_call(
        _dispatch_kernel,
        out_shape=(jax.ShapeDtypeStruct((n_rows, d), F32),
                   jax.ShapeDtypeStruct((nt, SUBLANES, LANES), I32)),
        grid_spec=grid_spec,
        compiler_params=_params("arbitrary"),
        name="moe_dispatch",
    )(zero_flags, xf, meta, pstart_lanes)


def _block_copy(hbm, blk, buf, slot, sem, to_hbm):
    if to_hbm:
        return pltpu.make_async_copy(buf.at[slot], _block_rows(hbm, blk), sem)
    return pltpu.make_async_copy(_block_rows(hbm, blk), buf.at[slot], sem)


def _ffn_kernel(pe_ref, pb_ref, pn_ref, pz_ref, xs_ref, wg_ref, wu_ref, wd_ref, ys_ref,
                xbuf, acc, zbuf, in_sem, out_sem, zero_sem, *, n_ftiles):
    del pe_ref
    p = pl.program_id(0)
    j = pl.program_id(1)
    nblk = pn_ref[p]
    nzero = pz_ref[p]
    blk0 = pb_ref[p]

    def for_blocks(count, fn):
        for s in range(PASS_BLOCKS):
            pl.when(s < count)(functools.partial(fn, s))

    for_valid_blocks = functools.partial(for_blocks, nblk)

    @pl.when((p == 0) & (j == 0))
    def _():
        zbuf[...] = jnp.zeros_like(zbuf)

    @pl.when(j == 0)
    def _():
        def zero_copy(s):
            return pltpu.make_async_copy(zbuf, _block_rows(ys_ref, blk0 + s), zero_sem)

        for_blocks(nzero, lambda s: zero_copy(s).start())
        for_blocks(nzero, lambda s: zero_copy(s).wait())

    @pl.when(j == 0)
    def _():
        for_valid_blocks(lambda s: _block_copy(xs_ref, blk0 + s, xbuf, s, in_sem, False).start())
        for_valid_blocks(lambda s: _block_copy(xs_ref, blk0 + s, xbuf, s, in_sem, False).wait())

    def compute(s):
        x = xbuf[s]
        gate = jnp.dot(x, wg_ref[...], preferred_element_type=F32)
        up = jnp.dot(x, wu_ref[...], preferred_element_type=F32)
        hmid = gate * _sigmoid(gate) * up
        y = jnp.dot(hmid, wd_ref[...], preferred_element_type=F32)

        @pl.when(j == 0)
        def _():
            acc[s] = y

        @pl.when(j > 0)
        def _():
            acc[s] = acc[s] + y

    for_valid_blocks(compute)

    @pl.when(j == n_ftiles - 1)
    def _():
        for_valid_blocks(lambda s: _block_copy(ys_ref, blk0 + s, acc, s, out_sem, True).start())
        for_valid_blocks(lambda s: _block_copy(ys_ref, blk0 + s, acc, s, out_sem, True).wait())


def _ffn(xs, w_gate, w_up, w_down, pass_expert, pass_block, pass_nblk, pass_nzero):
    n_rows, d = xs.shape
    n_exp, _, de = w_gate.shape
    tf = _largest_tile(de, 256, LANES)
    n_ftiles = de // tf
    n_pass = pass_expert.shape[0]
    last = n_ftiles - 1

    def f_idx(p, j, pn):
        return jnp.where(pn[p] > 0, j, last)

    grid_spec = pltpu.PrefetchScalarGridSpec(
        num_scalar_prefetch=4,
        grid=(n_pass, n_ftiles),
        in_specs=[pl.BlockSpec(memory_space=pl.ANY),
                  pl.BlockSpec((None, d, tf), lambda p, j, pe, pb, pn, pz: (pe[p], 0, f_idx(p, j, pn))),
                  pl.BlockSpec((None, d, tf), lambda p, j, pe, pb, pn, pz: (pe[p], 0, f_idx(p, j, pn))),
                  pl.BlockSpec((None, tf, d), lambda p, j, pe, pb, pn, pz: (pe[p], f_idx(p, j, pn), 0))],
        out_specs=pl.BlockSpec(memory_space=pl.ANY),
        scratch_shapes=[pltpu.VMEM((PASS_BLOCKS, MOE_BLOCK, d), F32),
                        pltpu.VMEM((PASS_BLOCKS, MOE_BLOCK, d), F32),
                        pltpu.VMEM((MOE_BLOCK, d), F32),
                        pltpu.SemaphoreType.DMA,
                        pltpu.SemaphoreType.DMA,
                        pltpu.SemaphoreType.DMA])
    return pl.pallas_call(
        functools.partial(_ffn_kernel, n_ftiles=n_ftiles),
        out_shape=jax.ShapeDtypeStruct((n_rows, d), F32),
        grid_spec=grid_spec,
        compiler_params=_params("arbitrary", "arbitrary"),
        name="moe_ffn",
    )(pass_expert, pass_block, pass_nblk, pass_nzero, xs, w_gate, w_up, w_down)


def _combine_kernel(h_ref, gate_ref, dest_ref, ys_ref, o_ref, ybuf, dest_smem, sem, row_sem):
    tm = h_ref.shape[0]
    cp = pltpu.make_async_copy(dest_ref.at[0], dest_smem, sem)
    cp.start()
    cp.wait()

    def issue(tok, carry):
        for k in range(TOP_K):
            _row_copy(ys_ref, dest_smem[k, tok], ybuf.at[k], tok, row_sem).start()
        return carry

    lax.fori_loop(0, tm, issue, 0)

    def drain(tok, carry):
        for k in range(TOP_K):
            _row_copy(ys_ref, 0, ybuf.at[k], 0, row_sem).wait()
        return carry

    lax.fori_loop(0, tm, drain, 0)

    gates = gate_ref[...]
    f = ybuf[0] * gates[:, 0:1]
    for k in range(1, TOP_K):
        f = f + ybuf[k] * gates[:, k:k + 1]
    o_ref[...] = h_ref[...] + f


def _combine(h, gates, dest, ys):
    t, d = h.shape
    tm = CHUNK
    return pl.pallas_call(
        _combine_kernel,
        out_shape=jax.ShapeDtypeStruct((t, d), F32),
        grid=(t // tm,),
        in_specs=[pl.BlockSpec((tm, d), lambda i: (i, 0)),
                  pl.BlockSpec((tm, LANES), lambda i: (i, 0)),
                  pl.BlockSpec((1, SUBLANES, LANES), lambda i: (i, 0, 0)),
                  pl.BlockSpec(memory_space=pl.ANY)],
        out_specs=pl.BlockSpec((tm, d), lambda i: (i, 0)),
        scratch_shapes=[pltpu.VMEM((TOP_K, tm, d), F32),
                        pltpu.SMEM((SUBLANES, LANES), I32),
                        pltpu.SemaphoreType.DMA,
                        pltpu.SemaphoreType.DMA],
        compiler_params=_params("arbitrary"),
        name="moe_combine",
    )(h, gates, dest, ys)


def _pass_tables(counts, n_blocks, n_pass):
    blocks = (counts + MOE_BLOCK - 1) // MOE_BLOCK
    block_end = jnp.cumsum(blocks)
    block_start = block_end - blocks
    used = block_end[-1]
    passes = (blocks + PASS_BLOCKS - 1) // PASS_BLOCKS
    pass_end = jnp.cumsum(passes)
    pass_start = pass_end - passes
    total = pass_end[-1]
    p = jnp.arange(n_pass, dtype=I32)
    valid = p < total
    pc = jnp.where(valid, p, jnp.maximum(total - 1, 0))
    e = jnp.minimum(jnp.searchsorted(pass_end, pc, side="right"), N_EXPERTS - 1).astype(I32)
    k = pc - pass_start[e]
    tail_first = used + (p - total) * PASS_BLOCKS
    first = jnp.where(valid, block_start[e] + k * PASS_BLOCKS, tail_first)
    nblk = jnp.where(valid, jnp.clip(blocks[e] - k * PASS_BLOCKS, 0, PASS_BLOCKS), 0)
    nzero = jnp.where(valid, 0, jnp.clip(n_blocks - tail_first, 0, PASS_BLOCKS))
    blk = jnp.arange(n_blocks, dtype=I32)
    ragged_last = jnp.where(counts % MOE_BLOCK != 0, block_end - 1, n_blocks)
    zero_flags = (blk >= used) | jnp.zeros((n_blocks,), bool).at[ragged_last].set(True, mode="drop")
    return (e, first.astype(I32), nblk.astype(I32), nzero.astype(I32),
            (block_start * MOE_BLOCK).astype(I32), zero_flags.astype(I32))


def _hier_moe(h, norm_g, w_group, b_group, w_expert, b_expert, w_gate, w_up, w_down):
    t, d = h.shape
    n_assign = t * TOP_K
    n_blocks = -(-(n_assign + N_EXPERTS * (MOE_BLOCK - 1)) // MOE_BLOCK)
    n_pass = n_blocks // PASS_BLOCKS + N_EXPERTS + 1
    xf, meta, gates, cnt = _router(h, norm_g, w_group, b_group, w_expert, b_expert)
    counts = cnt[0, EXPERT_LANE0:EXPERT_LANE0 + N_EXPERTS].astype(I32)
    pass_expert, pass_block, pass_nblk, pass_nzero, row_start, zero_flags = _pass_tables(
        counts, n_blocks, n_pass)
    pstart_lanes = jnp.zeros((1, LANES), F32).at[0, EXPERT_LANE0:EXPERT_LANE0 + N_EXPERTS].set(
        row_start.astype(F32))
    xs, dest = _dispatch(xf, meta, pstart_lanes, zero_flags)
    ys = _ffn(xs, w_gate, w_up, w_down, pass_expert, pass_block, pass_nblk, pass_nzero)
    return _combine(h, gates, dest, ys)


def kernel(x_prompt, x_sample, state_conv, mix_norm_g, sgu_w_in, sgu_b_in, sgu_v_norm_g, sgu_v_norm_b, sgu_w_s, sgu_b_s, sgu_w_out, sgu_b_out, conv_w_pw1, conv_b_pw1, conv_w_dw, conv_b_dw, conv_ln_g, conv_ln_b, conv_w_pw2, conv_b_pw2, ffn_norm_g, moe_w_group, moe_b_group, moe_w_expert, moe_b_expert, moe_w_gate, moe_w_up, moe_w_down, final_norm_g):
    batch, seq, d = x_prompt.shape
    dec_batch, dec_seq, _ = x_sample.shape
    assert dec_seq == 1
    n_p = batch * seq
    depth = mix_norm_g.shape[0]
    h = jnp.concatenate([x_prompt.reshape(n_p, d), x_sample.reshape(dec_batch, d)], axis=0)

    sgu_v_new, conv_p_new, conv_s_new = [], [], []
    for i in range(depth):
        j = i // 2
        hn = _rmsnorm(h, mix_norm_g[i], BF16)
        if i % 2 == 0:
            z = _mm_gelu(hn, _cast_bf16(sgu_w_in[j]), sgu_b_in[j])
            us, v_s = _sgu_mid(z, sgu_v_norm_g[j], sgu_v_norm_b[j], sgu_w_s[j], sgu_b_s[j], n_p)
            h = _mm_res(us, _cast_bf16(sgu_w_out[j]), sgu_b_out[j], h)
            sgu_v_new.append(v_s.reshape(dec_batch, dec_seq, -1))
        else:
            glu = _mm_glu(hn, _cast_bf16(conv_w_pw1[j]), conv_b_pw1[j])
            conv_args = (conv_w_dw[j], conv_b_dw[j], conv_ln_g[j], conv_ln_b[j])
            y_s, c_s = _conv_sample(state_conv[j], glu, *conv_args, n_p)
            yc = _conv_prompt(glu, y_s, *conv_args, n_p, seq)
            h = _mm_res(yc, _cast_bf16(conv_w_pw2[j]), conv_b_pw2[j], h)
            conv_p_new.append(glu[:n_p].reshape(batch, seq, -1)[:, seq - (CONV_WIDTH - 1):])
            conv_s_new.append(c_s)
        h = _hier_moe(h, ffn_norm_g[i], moe_w_group[i], moe_b_group[i], moe_w_expert[i],
                      moe_b_expert[i], moe_w_gate[i], moe_w_up[i], moe_w_down[i])

    y_prompt = _rmsnorm(h, final_norm_g, F32, 0, n_p).reshape(batch, seq, d)
    y_sample = _rmsnorm(h, final_norm_g, F32, n_p, dec_batch).reshape(dec_batch, dec_seq, d)
    return (y_prompt, y_sample, jnp.stack(sgu_v_new), jnp.stack(conv_p_new), jnp.stack(conv_s_new))
```

```python
import functools

import jax
import jax.numpy as jnp
from jax import lax
from jax.experimental import pallas as pl
from jax.experimental.pallas import tpu as pltpu

F32 = jnp.float32
BF16 = jnp.bfloat16
I32 = jnp.int32

EPS = 1e-6
CHUNK = 128
LANES = 128
SUBLANES = 8
CONV_WIDTH = 31
N_EXPERT_GROUPS = 8
EXPERTS_PER_GROUP = 8
N_EXPERTS = N_EXPERT_GROUPS * EXPERTS_PER_GROUP
TOP_K = 2
MOE_BLOCK = 128
PASS_BLOCKS = 3
EXPERT_LANE0 = N_EXPERT_GROUPS
VMEM_LIMIT_BYTES = 56 * 1024 * 1024


def _params(*sem):
    return pltpu.CompilerParams(dimension_semantics=sem, vmem_limit_bytes=VMEM_LIMIT_BYTES)


def _largest_tile(n, cap, mult):
    t = (min(cap, n) // mult) * mult
    while t > mult and n % t:
        t -= mult
    assert t >= mult and n % t == 0, (n, cap, mult)
    return t


def _cast_kernel(x_ref, o_ref):
    o_ref[...] = x_ref[...].astype(o_ref.dtype)


def _cast_bf16(w):
    k, n = w.shape
    tk = _largest_tile(k, 256, 16)
    return pl.pallas_call(
        _cast_kernel,
        out_shape=jax.ShapeDtypeStruct((k, n), BF16),
        grid=(k // tk,),
        in_specs=[pl.BlockSpec((tk, n), lambda i: (i, 0))],
        out_specs=pl.BlockSpec((tk, n), lambda i: (i, 0)),
        compiler_params=_params("arbitrary"),
        name="cast_bf16",
    )(w)


def _sigmoid(x):
    return 1.0 / (1.0 + jnp.exp(-x))


def _mm_gelu_kernel(x_ref, w_ref, b_ref, o_ref):
    acc = jnp.dot(x_ref[...], w_ref[...], preferred_element_type=F32) + b_ref[...]
    o_ref[...] = jax.nn.gelu(acc).astype(o_ref.dtype)


def _mm_glu_kernel(x_ref, wa_ref, wg_ref, ba_ref, bg_ref, o_ref):
    x = x_ref[...]
    a = jnp.dot(x, wa_ref[...], preferred_element_type=F32) + ba_ref[...]
    gate = jnp.dot(x, wg_ref[...], preferred_element_type=F32) + bg_ref[...]
    o_ref[...] = (a * _sigmoid(gate)).astype(o_ref.dtype)


def _mm_res_kernel(x_ref, w_ref, b_ref, r_ref, o_ref):
    acc = jnp.dot(x_ref[...], w_ref[...], preferred_element_type=F32) + b_ref[...]
    o_ref[...] = (r_ref[...] + acc).astype(o_ref.dtype)


def _mm_tiles(t, n):
    return _largest_tile(t, 832, 16), _largest_tile(n, 1024, LANES)


def _mm_gelu(x, w, b):
    t, k = x.shape
    n = w.shape[1]
    tm, tn = _mm_tiles(t, n)
    return pl.pallas_call(
        _mm_gelu_kernel,
        out_shape=jax.ShapeDtypeStruct((t, n), F32),
        grid=(t // tm, n // tn),
        in_specs=[pl.BlockSpec((tm, k), lambda i, j: (i, 0)),
                  pl.BlockSpec((k, tn), lambda i, j: (0, j)),
                  pl.BlockSpec((1, tn), lambda i, j: (0, j))],
        out_specs=pl.BlockSpec((tm, tn), lambda i, j: (i, j)),
        compiler_params=_params("arbitrary", "arbitrary"),
        name="mm_gelu",
    )(x, w, b.reshape(1, n))


def _mm_glu(x, w, b):
    t, k = x.shape
    n = w.shape[1] // 2
    tm, tn = _mm_tiles(t, n)
    tn = min(tn, 512)
    nb = n // tn
    b2 = b.reshape(1, 2 * n)
    return pl.pallas_call(
        _mm_glu_kernel,
        out_shape=jax.ShapeDtypeStruct((t, n), F32),
        grid=(t // tm, nb),
        in_specs=[pl.BlockSpec((tm, k), lambda i, j: (i, 0)),
                  pl.BlockSpec((k, tn), lambda i, j: (0, j)),
                  pl.BlockSpec((k, tn), lambda i, j: (0, j + nb)),
                  pl.BlockSpec((1, tn), lambda i, j: (0, j)),
                  pl.BlockSpec((1, tn), lambda i, j: (0, j + nb))],
        out_specs=pl.BlockSpec((tm, tn), lambda i, j: (i, j)),
        compiler_params=_params("arbitrary", "arbitrary"),
        name="mm_glu",
    )(x, w, w, b2, b2)


def _mm_res(x, w, b, res):
    t, k = x.shape
    n = w.shape[1]
    tm, tn = _mm_tiles(t, n)
    return pl.pallas_call(
        _mm_res_kernel,
        out_shape=jax.ShapeDtypeStruct((t, n), F32),
        grid=(t // tm, n // tn),
        in_specs=[pl.BlockSpec((tm, k), lambda i, j: (i, 0)),
                  pl.BlockSpec((k, tn), lambda i, j: (0, j)),
                  pl.BlockSpec((1, tn), lambda i, j: (0, j)),
                  pl.BlockSpec((tm, tn), lambda i, j: (i, j))],
        out_specs=pl.BlockSpec((tm, tn), lambda i, j: (i, j)),
        compiler_params=_params("arbitrary", "arbitrary"),
        name="mm_res",
    )(x, w, b.reshape(1, n), res)


def _layer_norm(x, g, b):
    xc = x - jnp.mean(x, axis=-1, keepdims=True)
    y = xc * lax.rsqrt(jnp.mean(xc * xc, axis=-1, keepdims=True) + EPS)
    return y * g + b


def _sgu_mid_kernel(u_ref, v_ref, g_ref, b_ref, ws_ref, bias_ref, w00_ref, b00_ref,
                    us_ref, vs_ref, wt_ref, *, n_groups, n_prompt_tiles):
    i = pl.program_id(0)

    @pl.when(i == 0)
    def _():
        r = lax.broadcasted_iota(I32, (CHUNK, CHUNK), 0)
        c = lax.broadcasted_iota(I32, (CHUNK, CHUNK), 1)
        tril = (c <= r).astype(F32)
        for g in range(n_groups):
            wt_ref[g] = (ws_ref[g] * tril).astype(BF16)

    vn = _layer_norm(v_ref[...], g_ref[...], b_ref[...])

    @pl.when(i < n_prompt_tiles)
    def _():
        vb = vn.astype(BF16)
        for g in range(n_groups):
            sl = slice(g * LANES, (g + 1) * LANES)
            s = jnp.dot(wt_ref[g], vb[:, sl], preferred_element_type=F32) + bias_ref[:, sl]
            us_ref[:, sl] = (u_ref[:, sl] * s).astype(us_ref.dtype)

    @pl.when(i >= n_prompt_tiles)
    def _():
        s = vn * w00_ref[...] + b00_ref[...]
        us_ref[...] = (u_ref[...] * s).astype(us_ref.dtype)
        vs_ref[...] = vn


def _sgu_mid(z, v_g, v_b, w_s, b_s, n_prompt_rows):
    t, w2 = z.shape
    w = w2 // 2
    n_groups = w // LANES
    assert w_s.shape == (n_groups, CHUNK, CHUNK) and t % CHUNK == 0 and n_prompt_rows % CHUNK == 0
    npt = n_prompt_rows // CHUNK
    nst = t // CHUNK - npt
    bias_full = jnp.repeat(b_s.T, LANES, axis=1)
    w00 = jnp.repeat(w_s[:, 0, 0], LANES).reshape(1, w)
    b00 = jnp.repeat(b_s[:, 0], LANES).reshape(1, w)
    kern = functools.partial(_sgu_mid_kernel, n_groups=n_groups, n_prompt_tiles=npt)
    return pl.pallas_call(
        kern,
        out_shape=(jax.ShapeDtypeStruct((t, w), BF16),
                   jax.ShapeDtypeStruct((nst * CHUNK, w), F32)),
        grid=(t // CHUNK,),
        in_specs=[pl.BlockSpec((CHUNK, w), lambda i: (i, 0)),
                  pl.BlockSpec((CHUNK, w), lambda i: (i, 1)),
                  pl.BlockSpec((1, w), lambda i: (0, 0)),
                  pl.BlockSpec((1, w), lambda i: (0, 0)),
                  pl.BlockSpec((n_groups, CHUNK, CHUNK), lambda i: (0, 0, 0)),
                  pl.BlockSpec((CHUNK, w), lambda i: (0, 0)),
                  pl.BlockSpec((1, w), lambda i: (0, 0)),
                  pl.BlockSpec((1, w), lambda i: (0, 0))],
        out_specs=(pl.BlockSpec((CHUNK, w), lambda i: (i, 0)),
                   pl.BlockSpec((CHUNK, w), lambda i: (jnp.maximum(i - npt, 0), 0))),
        scratch_shapes=[pltpu.VMEM((n_groups, CHUNK, CHUNK), BF16)],
        compiler_params=_params("arbitrary"),
        name="sgu_mid",
    )(z, z, v_g.reshape(1, w), v_b.reshape(1, w), w_s, bias_full, w00, b00)


CONV_HALO = 32
CONV_TILE = 128


def _conv_prompt_kernel(x_ref, halo_ref, ysamp_ref, w_ref, bdw_ref, g_ref, b_ref, o_ref,
                        ext_ref, sh_ref, y_ref, *, tiles_per_seq, n_prompt_tiles):
    i = pl.program_id(0)
    n_strips = ext_ref.shape[0]
    ext_rows = CONV_HALO + CONV_TILE
    sh_rows = sh_ref.shape[1]
    first = CONV_HALO - (CONV_WIDTH - 1)

    @pl.when(i >= n_prompt_tiles)
    def _():
        o_ref[...] = ysamp_ref[...]

    @pl.when(i < n_prompt_tiles)
    def _():
        @pl.when(i % tiles_per_seq == 0)
        def _():
            ext_ref[:, 0:CONV_HALO, :] = jnp.zeros((n_strips, CONV_HALO, LANES), F32)

        @pl.when(i % tiles_per_seq != 0)
        def _():
            for s in range(n_strips):
                ext_ref[s, 0:CONV_HALO, :] = halo_ref[:, s * LANES:(s + 1) * LANES]

        for s in range(n_strips):
            ext_ref[s, CONV_HALO:ext_rows, :] = x_ref[:, s * LANES:(s + 1) * LANES]

        def strip(s, carry):
            for r in range(1, SUBLANES):
                sh_ref[r - 1] = ext_ref[s, r:r + sh_rows, :]
            taps_w = [jnp.broadcast_to(w_ref[s, k:k + 1, :], (SUBLANES, LANES)) for k in range(CONV_WIDTH)]
            bias = jnp.broadcast_to(bdw_ref[s], (SUBLANES, LANES))
            for rb in range(CONV_TILE // SUBLANES):
                acc = bias
                for k in range(CONV_WIDTH):
                    m, r = divmod(first + k, SUBLANES)
                    rows = slice((rb + m) * SUBLANES, (rb + m + 1) * SUBLANES)
                    tap = ext_ref[s, rows, :] if r == 0 else sh_ref[r - 1, rows, :]
                    acc = acc + tap * taps_w[k]
                y_ref[s, rb * SUBLANES:(rb + 1) * SUBLANES, :] = acc
            return carry

        lax.fori_loop(0, n_strips, strip, 0)

        inv_c = 1.0 / (n_strips * LANES)
        total = jnp.zeros((CONV_TILE, 1), F32)
        for s in range(n_strips):
            total = total + jnp.sum(y_ref[s], axis=-1, keepdims=True)
        mean = total * inv_c
        sq = jnp.zeros((CONV_TILE, 1), F32)
        for s in range(n_strips):
            yc = y_ref[s] - mean
            sq = sq + jnp.sum(yc * yc, axis=-1, keepdims=True)
        rstd = lax.rsqrt(sq * inv_c + EPS)
        for s in range(n_strips):
            cols = slice(s * LANES, (s + 1) * LANES)
            y = (y_ref[s] - mean) * rstd * g_ref[:, cols] + b_ref[:, cols]
            o_ref[:, cols] = (y * _sigmoid(y)).astype(o_ref.dtype)


def _conv_prompt(glu, y_sample, w_dw, b_dw, ln_g, ln_b, n_prompt_rows, seq):
    t, c = glu.shape
    assert seq % CONV_TILE == 0 and n_prompt_rows % seq == 0 and t % CONV_TILE == 0
    assert y_sample.shape == (t - n_prompt_rows, c) and c % LANES == 0
    ratio = CONV_TILE // CONV_HALO
    npt = n_prompt_rows // CONV_TILE
    ns = c // LANES
    kern = functools.partial(_conv_prompt_kernel, tiles_per_seq=seq // CONV_TILE, n_prompt_tiles=npt)
    w_strips = w_dw.reshape(CONV_WIDTH, ns, LANES).transpose(1, 0, 2)

    def prompt_tile(i):
        return jnp.minimum(i, npt - 1)

    return pl.pallas_call(
        kern,
        out_shape=jax.ShapeDtypeStruct((t, c), BF16),
        grid=(t // CONV_TILE,),
        in_specs=[pl.BlockSpec((CONV_TILE, c), lambda i: (prompt_tile(i), 0)),
                  pl.BlockSpec((CONV_HALO, c), lambda i: (jnp.maximum(prompt_tile(i) * ratio - 1, 0), 0)),
                  pl.BlockSpec((CONV_TILE, c), lambda i: (jnp.maximum(i - npt, 0), 0)),
                  pl.BlockSpec((ns, CONV_WIDTH, LANES), lambda i: (0, 0, 0)),
                  pl.BlockSpec((ns, 1, LANES), lambda i: (0, 0, 0)),
                  pl.BlockSpec((1, c), lambda i: (0, 0)),
                  pl.BlockSpec((1, c), lambda i: (0, 0))],
        out_specs=pl.BlockSpec((CONV_TILE, c), lambda i: (i, 0)),
        scratch_shapes=[pltpu.VMEM((ns, CONV_HALO + CONV_TILE, LANES), F32),
                        pltpu.VMEM((SUBLANES - 1, CONV_HALO + CONV_TILE - SUBLANES, LANES), F32),
                        pltpu.VMEM((ns, CONV_TILE, LANES), F32)],
        compiler_params=_params("arbitrary"),
        name="conv_prompt",
    )(glu, glu, y_sample, w_strips, b_dw.reshape(ns, 1, LANES), ln_g.reshape(1, c), ln_b.reshape(1, c))


CONV_SAMPLE_TILE = 16


def _conv_sample_kernel(hist_ref, x_ref, w_ref, bdw_ref, g_ref, b_ref, yc_ref, ns_ref):
    nh = CONV_WIDTH - 1
    hist = hist_ref[...]
    x = x_ref[...]
    y = jnp.sum(hist * w_ref[0:nh, :][None], axis=1) + x * w_ref[nh:nh + 1, :] + bdw_ref[...]
    y = _layer_norm(y, g_ref[...], b_ref[...])
    yc_ref[...] = (y * _sigmoid(y)).astype(yc_ref.dtype)
    ns_ref[:, 0:nh - 1, :] = hist_ref[:, 1:nh, :]
    for b in range(CONV_SAMPLE_TILE):
        ns_ref[b, nh - 1:nh, :] = x_ref[b:b + 1, :]


def _conv_sample(hist, glu, w_dw, b_dw, ln_g, ln_b, n_prompt_rows):
    t, c = glu.shape
    db, nh, _ = hist.shape
    tb = CONV_SAMPLE_TILE
    assert nh == CONV_WIDTH - 1 and db % tb == 0 and n_prompt_rows % tb == 0 and t == n_prompt_rows + db
    b0 = n_prompt_rows // tb
    return pl.pallas_call(
        _conv_sample_kernel,
        out_shape=(jax.ShapeDtypeStruct((db, c), BF16),
                   jax.ShapeDtypeStruct((db, nh, c), F32)),
        grid=(db // tb,),
        in_specs=[pl.BlockSpec((tb, nh, c), lambda i: (i, 0, 0)),
                  pl.BlockSpec((tb, c), lambda i: (i + b0, 0)),
                  pl.BlockSpec((CONV_WIDTH, c), lambda i: (0, 0)),
                  pl.BlockSpec((1, c), lambda i: (0, 0)),
                  pl.BlockSpec((1, c), lambda i: (0, 0)),
                  pl.BlockSpec((1, c), lambda i: (0, 0))],
        out_specs=(pl.BlockSpec((tb, c), lambda i: (i, 0)),
                   pl.BlockSpec((tb, nh, c), lambda i: (i, 0, 0))),
        compiler_params=_params("arbitrary"),
        name="conv_sample",
    )(hist, glu, w_dw, b_dw.reshape(1, c), ln_g.reshape(1, c), ln_b.reshape(1, c))


ROUTER_TILE = 320


def _pick_lanes(lane, *cols):
    out = jnp.zeros((cols[0].shape[0], LANES), cols[0].dtype)
    for k, col in enumerate(cols):
        out = jnp.where(lane == k, col, out)
    return out


def _router_kernel(h_ref, g_ref, wr_ref, br_ref, xf_ref, meta_ref, gate_ref, cnt_ref, carry_ref):
    i = pl.program_id(0)
    tm = h_ref.shape[0]

    @pl.when(i == 0)
    def _():
        carry_ref[...] = jnp.zeros_like(carry_ref)

    x = h_ref[...]
    xf = x * lax.rsqrt(jnp.mean(x * x, axis=-1, keepdims=True) + EPS) * g_ref[...]
    xf_ref[...] = xf
    logits = jnp.dot(xf, wr_ref[...], preferred_element_type=F32,
                     precision=lax.Precision.HIGHEST) + br_ref[...]
    lane = lax.broadcasted_iota(I32, (tm, LANES), 1)
    lane_f = lane.astype(F32)
    neg_inf = jnp.float32(-jnp.inf)

    def first_lane(hit):
        return jnp.min(jnp.where(hit, lane_f, float(LANES)), axis=-1, keepdims=True).astype(I32)

    is_g = lane < N_EXPERT_GROUPS
    gl = jnp.where(is_g, logits, neg_inf)
    gmax = jnp.max(gl, axis=-1, keepdims=True)
    grp = first_lane(gl == gmax)
    p_grp = 1.0 / jnp.sum(jnp.where(is_g, jnp.exp(logits - gmax), 0.0), axis=-1, keepdims=True)

    lo = EXPERT_LANE0 + grp * EXPERTS_PER_GROUP
    in_grp = (lane >= lo) & (lane < lo + EXPERTS_PER_GROUP)
    el = jnp.where(in_grp, logits, neg_inf)
    v0 = jnp.max(el, axis=-1, keepdims=True)
    j0 = first_lane(el == v0)
    el1 = jnp.where(lane == j0, neg_inf, el)
    v1 = jnp.max(el1, axis=-1, keepdims=True)
    j1 = first_lane(el1 == v1)
    e1 = jnp.exp(v1 - v0)
    gate0 = p_grp * (1.0 / (1.0 + e1))
    gate1 = p_grp * (e1 / (1.0 + e1))

    oh0 = lane == j0
    oh1 = lane == j1
    cnt = jnp.where(oh0 | oh1, 1.0, 0.0)
    r = lax.broadcasted_iota(I32, (tm, tm), 0)
    c = lax.broadcasted_iota(I32, (tm, tm), 1)
    strict_lower = jnp.where(c < r, 1.0, 0.0).astype(BF16)
    before = jnp.dot(strict_lower, cnt.astype(BF16), preferred_element_type=F32) + carry_ref[...]
    rank0 = jnp.sum(jnp.where(oh0, before, 0.0), axis=-1, keepdims=True).astype(I32)
    rank1 = jnp.sum(jnp.where(oh1, before, 0.0), axis=-1, keepdims=True).astype(I32)
    carry_ref[...] = carry_ref[...] + jnp.sum(cnt, axis=0, keepdims=True)
    cnt_ref[...] = carry_ref[...]

    meta_ref[...] = _pick_lanes(lane, j0 - EXPERT_LANE0, j1 - EXPERT_LANE0, rank0, rank1)
    gate_ref[...] = _pick_lanes(lane, gate0, gate1)


def _router(h, g, w_group, b_group, w_expert, b_expert):
    t, d = h.shape
    tm = _largest_tile(t, ROUTER_TILE, SUBLANES)
    pad = LANES - N_EXPERT_GROUPS - N_EXPERTS
    wr = jnp.concatenate([w_group, w_expert, jnp.zeros((d, pad), F32)], axis=1)
    br = jnp.concatenate([b_group, b_expert, jnp.zeros((pad,), F32)]).reshape(1, LANES)
    return pl.pallas_call(
        _router_kernel,
        out_shape=(jax.ShapeDtypeStruct((t, d), F32),
                   jax.ShapeDtypeStruct((t, LANES), I32),
                   jax.ShapeDtypeStruct((t, LANES), F32),
                   jax.ShapeDtypeStruct((1, LANES), F32)),
        grid=(t // tm,),
        in_specs=[pl.BlockSpec((tm, d), lambda i: (i, 0)),
                  pl.BlockSpec((1, d), lambda i: (0, 0)),
                  pl.BlockSpec((d, LANES), lambda i: (0, 0)),
                  pl.BlockSpec((1, LANES), lambda i: (0, 0))],
        out_specs=(pl.BlockSpec((tm, d), lambda i: (i, 0)),
                   pl.BlockSpec((tm, LANES), lambda i: (i, 0)),
                   pl.BlockSpec((tm, LANES), lambda i: (i, 0)),
                   pl.BlockSpec((1, LANES), lambda i: (0, 0))),
        scratch_shapes=[pltpu.VMEM((1, LANES), F32)],
        compiler_params=_params("arbitrary"),
        name="moe_router",
    )(h, g.reshape(1, d), wr, br)


def _row_copy(src, src_row, dst, dst_row, sem):
    return pltpu.make_async_copy(src.at[pl.ds(src_row, 1)], dst.at[pl.ds(dst_row, 1)], sem)


def _block_rows(ref, blk):
    return ref.at[pl.ds(pl.multiple_of(blk * MOE_BLOCK, MOE_BLOCK), MOE_BLOCK)]


def _dispatch_kernel(zflag_ref, xf_ref, meta_ref, pstart_ref, xs_ref, dest_ref,
                     dest_vmem, dest_smem, zbuf, sem, row_sem, zero_sem):
    tm = xf_ref.shape[0]

    @pl.when(pl.program_id(0) == 0)
    def _():
        zbuf[...] = jnp.zeros_like(zbuf)

        def zero_block(blk, carry, wait):
            @pl.when(zflag_ref[blk] != 0)
            def _():
                cp = pltpu.make_async_copy(zbuf, _block_rows(xs_ref, blk), zero_sem)
                if wait:
                    cp.wait()
                else:
                    cp.start()
            return carry

        n_blocks = zflag_ref.shape[0]
        lax.fori_loop(0, n_blocks, functools.partial(zero_block, wait=False), 0)
        lax.fori_loop(0, n_blocks, functools.partial(zero_block, wait=True), 0)

    lane = lax.broadcasted_iota(I32, (tm, LANES), 1)
    meta = meta_ref[...]
    pstart = pstart_ref[...]
    dests = []
    for k in range(TOP_K):
        e = meta[:, k:k + 1]
        rank = meta[:, TOP_K + k:TOP_K + k + 1]
        base = jnp.sum(jnp.where(lane - EXPERT_LANE0 == e, pstart, 0.0), axis=-1, keepdims=True)
        dests.append(base + rank.astype(F32))
    dest_t = jnp.transpose(_pick_lanes(lane, *dests))[0:SUBLANES, :].astype(I32)
    dest_ref[0] = dest_t
    dest_vmem[...] = dest_t
    cp = pltpu.make_async_copy(dest_vmem, dest_smem, sem)
    cp.start()
    cp.wait()

    def issue(tok, carry):
        for k in range(TOP_K):
            _row_copy(xf_ref, tok, xs_ref, dest_smem[k, tok], row_sem).start()
        return carry

    lax.fori_loop(0, tm, issue, 0)

    def drain(tok, carry):
        for k in range(TOP_K):
            _row_copy(xf_ref, 0, xs_ref, 0, row_sem).wait()
        return carry

    lax.fori_loop(0, tm, drain, 0)


def _dispatch(xf, meta, pstart_lanes, zero_flags):
    t, d = xf.shape
    tm = CHUNK
    nt = t // tm
    n_rows = zero_flags.shape[0] * MOE_BLOCK
    grid_spec = pltpu.PrefetchScalarGridSpec(
        num_scalar_prefetch=1,
        grid=(nt,),
        in_specs=[pl.BlockSpec((tm, d), lambda i, zf: (i, 0)),
                  pl.BlockSpec((tm, LANES), lambda i, zf: (i, 0)),
                  pl.BlockSpec((1, LANES), lambda i, zf: (0, 0))],
        out_specs=(pl.BlockSpec(memory_space=pl.ANY),
                   pl.BlockSpec((1, SUBLANES, LANES), lambda i, zf: (i, 0, 0))),
        scratch_shapes=[pltpu.VMEM((SUBLANES, LANES), I32),
                        pltpu.SMEM((SUBLANES, LANES), I32),
                        pltpu.VMEM((MOE_BLOCK, d), F32),
                        pltpu.SemaphoreType.DMA,
                        pltpu.SemaphoreType.DMA,
                        pltpu.SemaphoreType.DMA])
    return pl.pallas_call(
        _dispatch_kernel,
        out_shape=(jax.ShapeDtypeStruct((n_rows, d), F32),
                   jax.ShapeDtypeStruct((nt, SUBLANES, LANES), I32)),
        grid_spec=grid_spec,
        compiler_params=_params("arbitrary"),
        name="moe_dispatch",
    )(zero_flags, xf, meta, pstart_lanes)


FFN_DOWN_COLS = 1024


def _ffn_kernel(pe_ref, pb_ref, pn_ref, pz_ref, xs_ref, wg_ref, wu_ref, wd_ref, ys_ref,
                xbuf, acc, zbuf, in_sem, out_sem, zero_sem, *, n_ftiles, n_pass):
    del pe_ref
    p = pl.program_id(0)
    j = pl.program_id(1)
    d = acc.shape[1]
    nblk = pn_ref[p]
    slot = lax.rem(p, 2)
    p_next = jnp.minimum(p + 1, n_pass - 1)
    nblk_next = jnp.where(p + 1 < n_pass, pn_ref[p_next], 0)
    p_prev = jnp.maximum(p - 1, 0)
    nblk_prev = jnp.where(p > 0, pn_ref[p_prev], 0)

    def for_blocks(count, fn):
        for s in range(PASS_BLOCKS):
            pl.when(s < count)(functools.partial(fn, s))

    def block_slice(s):
        return pl.ds(s * MOE_BLOCK, MOE_BLOCK)

    def x_copy(q, s):
        q_slot = lax.rem(q, 2)
        return pltpu.make_async_copy(_block_rows(xs_ref, pb_ref[q] + s),
                                     xbuf.at[q_slot, block_slice(s)], in_sem.at[q_slot])

    def y_copy(q, s):
        return pltpu.make_async_copy(acc.at[block_slice(s)], _block_rows(ys_ref, pb_ref[q] + s), out_sem)

    def wait_prev_writeback():
        for_blocks(nblk_prev, lambda s: y_copy(p_prev, s).wait())

    @pl.when((p == 0) & (j == 0))
    def _():
        zbuf[...] = jnp.zeros_like(zbuf)
        acc[...] = jnp.zeros_like(acc)
        for_blocks(nblk, lambda s: x_copy(p, s).start())

    @pl.when(j == 0)
    def _():
        def zero_copy(s):
            return pltpu.make_async_copy(zbuf, _block_rows(ys_ref, pb_ref[p] + s), zero_sem)

        for_blocks(pz_ref[p], lambda s: zero_copy(s).start())
        for_blocks(pz_ref[p], lambda s: zero_copy(s).wait())
        for_blocks(nblk, lambda s: x_copy(p, s).wait())
        for_blocks(nblk_next, lambda s: x_copy(p_next, s).start())
        wait_prev_writeback()

    def compute(n):
        rows = n * MOE_BLOCK
        x = xbuf[slot, 0:rows, :].astype(BF16)
        gate = jnp.dot(x, wg_ref[...].astype(BF16), preferred_element_type=F32)
        up = jnp.dot(x, wu_ref[...].astype(BF16), preferred_element_type=F32)
        hmid = (gate * _sigmoid(gate) * up).astype(BF16)
        down_cols = min(FFN_DOWN_COLS, d)
        for c0 in range(0, d, down_cols):
            cols = slice(c0, c0 + down_cols)
            y = jnp.dot(hmid, wd_ref[:, cols].astype(BF16), preferred_element_type=F32)
            acc[0:rows, cols] = y + jnp.where(j == 0, 0.0, acc[0:rows, cols])

    for n in range(1, PASS_BLOCKS + 1):
        pl.when(nblk == n)(functools.partial(compute, n))

    @pl.when(j == n_ftiles - 1)
    def _():
        for_blocks(nblk, lambda s: y_copy(p, s).start())

        @pl.when(p == n_pass - 1)
        def _():
            for_blocks(nblk, lambda s: y_copy(p, s).wait())


def _ffn(xs, w_gate, w_up, w_down, layer, pass_expert, pass_block, pass_nblk, pass_nzero):
    n_rows, d = xs.shape
    de = w_gate.shape[-1]
    assert d % min(FFN_DOWN_COLS, d) == 0
    tf =_largest_tile(de, 256, LANES)
    n_ftiles = de // tf
    n_pass = pass_expert.shape[0]
    last = n_ftiles - 1

    def f_idx(p, j, pn):
        return jnp.where(pn[p] > 0, j, last)

    def in_proj(p, j, pe, pb, pn, pz):
        return (layer, pe[p], 0, f_idx(p, j, pn))

    def out_proj(p, j, pe, pb, pn, pz):
        return (layer, pe[p], f_idx(p, j, pn), 0)

    grid_spec = pltpu.PrefetchScalarGridSpec(
        num_scalar_prefetch=4,
        grid=(n_pass, n_ftiles),
        in_specs=[pl.BlockSpec(memory_space=pl.ANY),
                  pl.BlockSpec((None, None, d, tf), in_proj),
                  pl.BlockSpec((None, None, d, tf), in_proj),
                  pl.BlockSpec((None, None, tf, d), out_proj)],
        out_specs=pl.BlockSpec(memory_space=pl.ANY),
        scratch_shapes=[pltpu.VMEM((2, PASS_BLOCKS * MOE_BLOCK, d), F32),
                        pltpu.VMEM((PASS_BLOCKS * MOE_BLOCK, d), F32),
                        pltpu.VMEM((MOE_BLOCK, d), F32),
                        pltpu.SemaphoreType.DMA((2,)),
                        pltpu.SemaphoreType.DMA,
                        pltpu.SemaphoreType.DMA])
    return pl.pallas_call(
        functools.partial(_ffn_kernel, n_ftiles=n_ftiles, n_pass=n_pass),
        out_shape=jax.ShapeDtypeStruct((n_rows, d), F32),
        grid_spec=grid_spec,
        compiler_params=_params("arbitrary", "arbitrary"),
        name="moe_ffn",
    )(pass_expert, pass_block, pass_nblk, pass_nzero, xs, w_gate, w_up, w_down)


def _combine_kernel(h_ref, gate_ref, dest_ref, g_ref, ys_ref, out_a, out_b, ybuf, dest_smem, sem, row_sem,
                    *, final, n_prompt_tiles):
    tm = h_ref.shape[0]
    cp = pltpu.make_async_copy(dest_ref.at[0], dest_smem, sem)
    cp.start()
    cp.wait()

    def issue(tok, carry):
        for k in range(TOP_K):
            _row_copy(ys_ref, dest_smem[k, tok], ybuf.at[k], tok, row_sem).start()
        return carry

    lax.fori_loop(0, tm, issue, 0)

    def drain(tok, carry):
        for k in range(TOP_K):
            _row_copy(ys_ref, 0, ybuf.at[k], 0, row_sem).wait()
        return carry

    lax.fori_loop(0, tm, drain, 0)

    gates = gate_ref[...]
    f = ybuf[0] * gates[:, 0:1]
    for k in range(1, TOP_K):
        f = f + ybuf[k] * gates[:, k:k + 1]
    hn = h_ref[...] + f
    normed = hn * lax.rsqrt(jnp.mean(hn * hn, axis=-1, keepdims=True) + EPS) * g_ref[...]
    if final:
        i = pl.program_id(0)

        @pl.when(i < n_prompt_tiles)
        def _():
            out_a[...] = normed

        @pl.when(i >= n_prompt_tiles)
        def _():
            out_b[...] = normed
    else:
        out_a[...] = hn
        out_b[...] = normed.astype(out_b.dtype)


def _combine(h, gates, dest, ys, next_g, final, n_prompt_rows):
    t, d = h.shape
    tm = CHUNK
    assert n_prompt_rows % tm == 0 and t % tm == 0
    npt = n_prompt_rows // tm
    if final:
        out_shape = (jax.ShapeDtypeStruct((n_prompt_rows, d), F32),
                     jax.ShapeDtypeStruct((t - n_prompt_rows, d), F32))
        out_specs = (pl.BlockSpec((tm, d), lambda i: (jnp.minimum(i, npt - 1), 0)),
                     pl.BlockSpec((tm, d), lambda i: (jnp.maximum(i - npt, 0), 0)))
    else:
        out_shape = (jax.ShapeDtypeStruct((t, d), F32), jax.ShapeDtypeStruct((t, d), BF16))
        out_specs = (pl.BlockSpec((tm, d), lambda i: (i, 0)), pl.BlockSpec((tm, d), lambda i: (i, 0)))
    return pl.pallas_call(
        functools.partial(_combine_kernel, final=final, n_prompt_tiles=npt),
        out_shape=out_shape,
        grid=(t // tm,),
        in_specs=[pl.BlockSpec((tm, d), lambda i: (i, 0)),
                  pl.BlockSpec((tm, LANES), lambda i: (i, 0)),
                  pl.BlockSpec((1, SUBLANES, LANES), lambda i: (i, 0, 0)),
                  pl.BlockSpec((1, d), lambda i: (0, 0)),
                  pl.BlockSpec(memory_space=pl.ANY)],
        out_specs=out_specs,
        scratch_shapes=[pltpu.VMEM((TOP_K, tm, d), F32),
                        pltpu.SMEM((SUBLANES, LANES), I32),
                        pltpu.SemaphoreType.DMA,
                        pltpu.SemaphoreType.DMA],
        compiler_params=_params("arbitrary"),
        name="moe_combine",
    )(h, gates, dest, next_g.reshape(1, d), ys)


def _pass_tables(counts, n_blocks, n_pass):
    blocks = (counts + MOE_BLOCK - 1) // MOE_BLOCK
    block_end = jnp.cumsum(blocks)
    block_start = block_end - blocks
    used = block_end[-1]
    passes = (blocks + PASS_BLOCKS - 1) // PASS_BLOCKS
    pass_end = jnp.cumsum(passes)
    pass_start = pass_end - passes
    total = pass_end[-1]
    p = jnp.arange(n_pass, dtype=I32)
    valid = p < total
    pc = jnp.where(valid, p, jnp.maximum(total - 1, 0))
    e = jnp.minimum(jnp.searchsorted(pass_end, pc, side="right"), N_EXPERTS - 1).astype(I32)
    k = pc - pass_start[e]
    tail_first = used + (p - total) * PASS_BLOCKS
    first = jnp.where(valid, block_start[e] + k * PASS_BLOCKS, tail_first)
    nblk = jnp.where(valid, jnp.clip(blocks[e] - k * PASS_BLOCKS, 0, PASS_BLOCKS), 0)
    nzero = jnp.where(valid, 0, jnp.clip(n_blocks - tail_first, 0, PASS_BLOCKS))
    blk = jnp.arange(n_blocks, dtype=I32)
    ragged_last = jnp.where(counts % MOE_BLOCK != 0, block_end - 1, n_blocks)
    zero_flags = (blk >= used) | jnp.zeros((n_blocks,), bool).at[ragged_last].set(True, mode="drop")
    return (e, first.astype(I32), nblk.astype(I32), nzero.astype(I32),
            (block_start * MOE_BLOCK).astype(I32), zero_flags.astype(I32))


def _hier_moe(h, norm_g, w_group, b_group, w_expert, b_expert, w_gate, w_up, w_down, layer,
              next_g, final, n_prompt_rows):
    t, d = h.shape
    n_assign = t * TOP_K
    n_blocks = -(-(n_assign + N_EXPERTS * (MOE_BLOCK - 1)) // MOE_BLOCK)
    n_pass = n_blocks // PASS_BLOCKS + N_EXPERTS + 1
    xf, meta, gates, cnt = _router(h, norm_g, w_group, b_group, w_expert, b_expert)
    counts = cnt[0, EXPERT_LANE0:EXPERT_LANE0 + N_EXPERTS].astype(I32)
    pass_expert, pass_block, pass_nblk, pass_nzero, row_start, zero_flags = _pass_tables(
        counts, n_blocks, n_pass)
    pstart_lanes = jnp.zeros((1, LANES), F32).at[0, EXPERT_LANE0:EXPERT_LANE0 + N_EXPERTS].set(
        row_start.astype(F32))
    xs, dest = _dispatch(xf, meta, pstart_lanes, zero_flags)
    ys = _ffn(xs, w_gate, w_up, w_down, layer, pass_expert, pass_block, pass_nblk, pass_nzero)
    return _combine(h, gates, dest, ys, next_g, final, n_prompt_rows)


def _embed_kernel(xp_ref, xs_ref, g_ref, h_ref, n_ref, *, n_prompt_tiles):
    i = pl.program_id(0)

    def emit(x):
        h_ref[...] = x
        y = x * lax.rsqrt(jnp.mean(x * x, axis=-1, keepdims=True) + EPS)
        n_ref[...] = (y * g_ref[...]).astype(n_ref.dtype)

    @pl.when(i < n_prompt_tiles)
    def _():
        emit(xp_ref[...])

    @pl.when(i >= n_prompt_tiles)
    def _():
        emit(xs_ref[...])


def _embed(x_prompt, x_sample, g):
    n_p, d = x_prompt.shape
    n_s = x_sample.shape[0]
    tm = CHUNK
    assert n_p % tm == 0 and n_s % tm == 0
    npt = n_p // tm
    t = n_p + n_s
    return pl.pallas_call(
        functools.partial(_embed_kernel, n_prompt_tiles=npt),
        out_shape=(jax.ShapeDtypeStruct((t, d), F32), jax.ShapeDtypeStruct((t, d), BF16)),
        grid=(t // tm,),
        in_specs=[pl.BlockSpec((tm, d), lambda i: (jnp.minimum(i, npt - 1), 0)),
                  pl.BlockSpec((tm, d), lambda i: (jnp.maximum(i - npt, 0), 0)),
                  pl.BlockSpec((1, d), lambda i: (0, 0))],
        out_specs=(pl.BlockSpec((tm, d), lambda i: (i, 0)), pl.BlockSpec((tm, d), lambda i: (i, 0))),
        compiler_params=_params("arbitrary"),
        name="embed_norm",
    )(x_prompt, x_sample, g.reshape(1, d))


def kernel(x_prompt, x_sample, state_conv, mix_norm_g, sgu_w_in, sgu_b_in, sgu_v_norm_g, sgu_v_norm_b, sgu_w_s, sgu_b_s, sgu_w_out, sgu_b_out, conv_w_pw1, conv_b_pw1, conv_w_dw, conv_b_dw, conv_ln_g, conv_ln_b, conv_w_pw2, conv_b_pw2, ffn_norm_g, moe_w_group, moe_b_group, moe_w_expert, moe_b_expert, moe_w_gate, moe_w_up, moe_w_down, final_norm_g):
    batch, seq, d = x_prompt.shape
    dec_batch, dec_seq, _ = x_sample.shape
    assert dec_seq == 1
    n_p = batch * seq
    depth = mix_norm_g.shape[0]
    h, hn = _embed(x_prompt.reshape(n_p, d), x_sample.reshape(dec_batch, d), mix_norm_g[0])

    sgu_v_new, conv_p_new, conv_s_new = [], [], []
    for i in range(depth):
        j = i // 2
        final = i == depth - 1
        if i % 2 == 0:
            z = _mm_gelu(hn, _cast_bf16(sgu_w_in[j]), sgu_b_in[j])
            us, v_s = _sgu_mid(z, sgu_v_norm_g[j], sgu_v_norm_b[j], sgu_w_s[j], sgu_b_s[j], n_p)
            h = _mm_res(us, _cast_bf16(sgu_w_out[j]), sgu_b_out[j], h)
            sgu_v_new.append(v_s.reshape(dec_batch, dec_seq, -1))
        else:
            glu = _mm_glu(hn, _cast_bf16(conv_w_pw1[j]), conv_b_pw1[j])
            conv_args = (conv_w_dw[j], conv_b_dw[j], conv_ln_g[j], conv_ln_b[j])
            y_s, c_s = _conv_sample(state_conv[j], glu, *conv_args, n_p)
            yc = _conv_prompt(glu, y_s, *conv_args, n_p, seq)
            h = _mm_res(yc, _cast_bf16(conv_w_pw2[j]), conv_b_pw2[j], h)
            tail = CONV_WIDTH - 1
            conv_p_new.append(jnp.stack([glu[(b + 1) * seq - tail:(b + 1) * seq] for b in range(batch)]))
            conv_s_new.append(c_s)
        next_g = final_norm_g if final else mix_norm_g[i + 1]
        h, hn = _hier_moe(h, ffn_norm_g[i], moe_w_group[i], moe_b_group[i], moe_w_expert[i],
                          moe_b_expert[i], moe_w_gate, moe_w_up, moe_w_down, i, next_g, final, n_p)

    y_prompt = h.reshape(batch, seq, d)
    y_sample = hn.reshape(dec_batch, dec_seq, d)
    return (y_prompt, y_sample, jnp.stack(sgu_v_new), jnp.stack(conv_p_new), jnp.stack(conv_s_new))
```

```python
import functools

import jax
import jax.numpy as jnp
from jax import lax
from jax.experimental import pallas as pl
from jax.experimental.pallas import tpu as pltpu

F32 = jnp.float32
BF16 = jnp.bfloat16
I32 = jnp.int32

EPS = 1e-6
CHUNK = 128
LANES = 128
SUBLANES = 8
CONV_WIDTH = 31
N_EXPERT_GROUPS = 8
EXPERTS_PER_GROUP = 8
N_EXPERTS = N_EXPERT_GROUPS * EXPERTS_PER_GROUP
TOP_K = 2
MOE_BLOCK = 128
PASS_BLOCKS = 3
EXPERT_LANE0 = N_EXPERT_GROUPS
VMEM_LIMIT_BYTES = 56 * 1024 * 1024


def _params(*sem):
    return pltpu.CompilerParams(dimension_semantics=sem, vmem_limit_bytes=VMEM_LIMIT_BYTES)


def _largest_tile(n, cap, mult):
    t = (min(cap, n) // mult) * mult
    while t > mult and n % t:
        t -= mult
    assert t >= mult and n % t == 0, (n, cap, mult)
    return t


def _cast_kernel(x_ref, o_ref):
    o_ref[...] = x_ref[...].astype(o_ref.dtype)


def _cast_bf16(w):
    k, n = w.shape
    tk = _largest_tile(k, 256, 16)
    return pl.pallas_call(
        _cast_kernel,
        out_shape=jax.ShapeDtypeStruct((k, n), BF16),
        grid=(k // tk,),
        in_specs=[pl.BlockSpec((tk, n), lambda i: (i, 0))],
        out_specs=pl.BlockSpec((tk, n), lambda i: (i, 0)),
        compiler_params=_params("arbitrary"),
        name="cast_bf16",
    )(w)


def _sigmoid(x):
    return 1.0 / (1.0 + jnp.exp(-x))


def _mm_gelu_kernel(x_ref, w_ref, b_ref, o_ref):
    acc = jnp.dot(x_ref[...], w_ref[...], preferred_element_type=F32) + b_ref[...]
    o_ref[...] = jax.nn.gelu(acc).astype(o_ref.dtype)


def _mm_glu_kernel(x_ref, wa_ref, wg_ref, ba_ref, bg_ref, o_ref):
    x = x_ref[...]
    a = jnp.dot(x, wa_ref[...], preferred_element_type=F32) + ba_ref[...]
    gate = jnp.dot(x, wg_ref[...], preferred_element_type=F32) + bg_ref[...]
    o_ref[...] = (a * _sigmoid(gate)).astype(o_ref.dtype)


def _mm_res_kernel(x_ref, w_ref, b_ref, r_ref, o_ref):
    acc = jnp.dot(x_ref[...], w_ref[...], preferred_element_type=F32) + b_ref[...]
    o_ref[...] = (r_ref[...] + acc).astype(o_ref.dtype)


def _mm_tiles(t, n):
    return _largest_tile(t, 832, 16), _largest_tile(n, 1024, LANES)


def _mm_gelu(x, w, b):
    t, k = x.shape
    n = w.shape[1]
    tm, tn = _mm_tiles(t, n)
    return pl.pallas_call(
        _mm_gelu_kernel,
        out_shape=jax.ShapeDtypeStruct((t, n), F32),
        grid=(t // tm, n // tn),
        in_specs=[pl.BlockSpec((tm, k), lambda i, j: (i, 0)),
                  pl.BlockSpec((k, tn), lambda i, j: (0, j)),
                  pl.BlockSpec((1, tn), lambda i, j: (0, j))],
        out_specs=pl.BlockSpec((tm, tn), lambda i, j: (i, j)),
        compiler_params=_params("arbitrary", "arbitrary"),
        name="mm_gelu",
    )(x, w, b.reshape(1, n))


def _mm_glu(x, w, b):
    t, k = x.shape
    n = w.shape[1] // 2
    tm, tn = _mm_tiles(t, n)
    tn = min(tn, 512)
    nb = n // tn
    b2 = b.reshape(1, 2 * n)
    return pl.pallas_call(
        _mm_glu_kernel,
        out_shape=jax.ShapeDtypeStruct((t, n), F32),
        grid=(t // tm, nb),
        in_specs=[pl.BlockSpec((tm, k), lambda i, j: (i, 0)),
                  pl.BlockSpec((k, tn), lambda i, j: (0, j)),
                  pl.BlockSpec((k, tn), lambda i, j: (0, j + nb)),
                  pl.BlockSpec((1, tn), lambda i, j: (0, j)),
                  pl.BlockSpec((1, tn), lambda i, j: (0, j + nb))],
        out_specs=pl.BlockSpec((tm, tn), lambda i, j: (i, j)),
        compiler_params=_params("arbitrary", "arbitrary"),
        name="mm_glu",
    )(x, w, w, b2, b2)


def _mm_res(x, w, b, res):
    t, k = x.shape
    n = w.shape[1]
    tm, tn = _mm_tiles(t, n)
    return pl.pallas_call(
        _mm_res_kernel,
        out_shape=jax.ShapeDtypeStruct((t, n), F32),
        grid=(t // tm, n // tn),
        in_specs=[pl.BlockSpec((tm, k), lambda i, j: (i, 0)),
                  pl.BlockSpec((k, tn), lambda i, j: (0, j)),
                  pl.BlockSpec((1, tn), lambda i, j: (0, j)),
                  pl.BlockSpec((tm, tn), lambda i, j: (i, j))],
        out_specs=pl.BlockSpec((tm, tn), lambda i, j: (i, j)),
        compiler_params=_params("arbitrary", "arbitrary"),
        name="mm_res",
    )(x, w, b.reshape(1, n), res)


def _layer_norm(x, g, b):
    xc = x - jnp.mean(x, axis=-1, keepdims=True)
    y = xc * lax.rsqrt(jnp.mean(xc * xc, axis=-1, keepdims=True) + EPS)
    return y * g + b


def _sgu_mid_kernel(u_ref, v_ref, g_ref, b_ref, ws_ref, bias_ref, w00_ref, b00_ref,
                    us_ref, vs_ref, wt_ref, *, n_groups, n_prompt_tiles):
    i = pl.program_id(0)

    @pl.when(i == 0)
    def _():
        r = lax.broadcasted_iota(I32, (CHUNK, CHUNK), 0)
        c = lax.broadcasted_iota(I32, (CHUNK, CHUNK), 1)
        tril = (c <= r).astype(F32)
        for g in range(n_groups):
            wt_ref[g] = (ws_ref[g] * tril).astype(BF16)

    vn = _layer_norm(v_ref[...], g_ref[...], b_ref[...])

    @pl.when(i < n_prompt_tiles)
    def _():
        vb = vn.astype(BF16)
        for g in range(n_groups):
            sl = slice(g * LANES, (g + 1) * LANES)
            s = jnp.dot(wt_ref[g], vb[:, sl], preferred_element_type=F32) + bias_ref[:, sl]
            us_ref[:, sl] = (u_ref[:, sl] * s).astype(us_ref.dtype)

    @pl.when(i >= n_prompt_tiles)
    def _():
        s = vn * w00_ref[...] + b00_ref[...]
        us_ref[...] = (u_ref[...] * s).astype(us_ref.dtype)
        vs_ref[...] = vn


def _sgu_mid(z, v_g, v_b, w_s, b_s, n_prompt_rows):
    t, w2 = z.shape
    w = w2 // 2
    n_groups = w // LANES
    assert w_s.shape == (n_groups, CHUNK, CHUNK) and t % CHUNK == 0 and n_prompt_rows % CHUNK == 0
    npt = n_prompt_rows // CHUNK
    nst = t // CHUNK - npt
    bias_full = jnp.repeat(b_s.T, LANES, axis=1)
    w00 = jnp.repeat(w_s[:, 0, 0], LANES).reshape(1, w)
    b00 = jnp.repeat(b_s[:, 0], LANES).reshape(1, w)
    kern = functools.partial(_sgu_mid_kernel, n_groups=n_groups, n_prompt_tiles=npt)
    return pl.pallas_call(
        kern,
        out_shape=(jax.ShapeDtypeStruct((t, w), BF16),
                   jax.ShapeDtypeStruct((nst * CHUNK, w), F32)),
        grid=(t // CHUNK,),
        in_specs=[pl.BlockSpec((CHUNK, w), lambda i: (i, 0)),
                  pl.BlockSpec((CHUNK, w), lambda i: (i, 1)),
                  pl.BlockSpec((1, w), lambda i: (0, 0)),
                  pl.BlockSpec((1, w), lambda i: (0, 0)),
                  pl.BlockSpec((n_groups, CHUNK, CHUNK), lambda i: (0, 0, 0)),
                  pl.BlockSpec((CHUNK, w), lambda i: (0, 0)),
                  pl.BlockSpec((1, w), lambda i: (0, 0)),
                  pl.BlockSpec((1, w), lambda i: (0, 0))],
        out_specs=(pl.BlockSpec((CHUNK, w), lambda i: (i, 0)),
                   pl.BlockSpec((CHUNK, w), lambda i: (jnp.maximum(i - npt, 0), 0))),
        scratch_shapes=[pltpu.VMEM((n_groups, CHUNK, CHUNK), BF16)],
        compiler_params=_params("arbitrary"),
        name="sgu_mid",
    )(z, z, v_g.reshape(1, w), v_b.reshape(1, w), w_s, bias_full, w00, b00)


CONV_HALO = 32
CONV_TILE = 128


def _conv_prompt_kernel(x_ref, halo_ref, ysamp_ref, w_ref, bdw_ref, g_ref, b_ref, o_ref,
                        ext_ref, sh_ref, y_ref, *, tiles_per_seq, n_prompt_tiles):
    i = pl.program_id(0)
    n_strips = ext_ref.shape[0]
    ext_rows = CONV_HALO + CONV_TILE
    sh_rows = sh_ref.shape[1]
    first = CONV_HALO - (CONV_WIDTH - 1)

    @pl.when(i >= n_prompt_tiles)
    def _():
        o_ref[...] = ysamp_ref[...]

    @pl.when(i < n_prompt_tiles)
    def _():
        @pl.when(i % tiles_per_seq == 0)
        def _():
            ext_ref[:, 0:CONV_HALO, :] = jnp.zeros((n_strips, CONV_HALO, LANES), F32)

        @pl.when(i % tiles_per_seq != 0)
        def _():
            for s in range(n_strips):
                ext_ref[s, 0:CONV_HALO, :] = halo_ref[:, s * LANES:(s + 1) * LANES]

        for s in range(n_strips):
            ext_ref[s, CONV_HALO:ext_rows, :] = x_ref[:, s * LANES:(s + 1) * LANES]

        def strip(s, carry):
            for r in range(1, SUBLANES):
                sh_ref[r - 1] = ext_ref[s, r:r + sh_rows, :]
            taps_w = [jnp.broadcast_to(w_ref[s, k:k + 1, :], (SUBLANES, LANES)) for k in range(CONV_WIDTH)]
            bias = jnp.broadcast_to(bdw_ref[s], (SUBLANES, LANES))
            for rb in range(CONV_TILE // SUBLANES):
                acc = bias
                for k in range(CONV_WIDTH):
                    m, r = divmod(first + k, SUBLANES)
                    rows = slice((rb + m) * SUBLANES, (rb + m + 1) * SUBLANES)
                    tap = ext_ref[s, rows, :] if r == 0 else sh_ref[r - 1, rows, :]
                    acc = acc + tap * taps_w[k]
                y_ref[s, rb * SUBLANES:(rb + 1) * SUBLANES, :] = acc
            return carry

        lax.fori_loop(0, n_strips, strip, 0)

        inv_c = 1.0 / (n_strips * LANES)
        total = jnp.zeros((CONV_TILE, 1), F32)
        for s in range(n_strips):
            total = total + jnp.sum(y_ref[s], axis=-1, keepdims=True)
        mean = total * inv_c
        sq = jnp.zeros((CONV_TILE, 1), F32)
        for s in range(n_strips):
            yc = y_ref[s] - mean
            sq = sq + jnp.sum(yc * yc, axis=-1, keepdims=True)
        rstd = lax.rsqrt(sq * inv_c + EPS)
        for s in range(n_strips):
            cols = slice(s * LANES, (s + 1) * LANES)
            y = (y_ref[s] - mean) * rstd * g_ref[:, cols] + b_ref[:, cols]
            o_ref[:, cols] = (y * _sigmoid(y)).astype(o_ref.dtype)


def _conv_prompt(glu, y_sample, w_dw, b_dw, ln_g, ln_b, n_prompt_rows, seq):
    t, c = glu.shape
    assert seq % CONV_TILE == 0 and n_prompt_rows % seq == 0 and t % CONV_TILE == 0
    assert y_sample.shape == (t - n_prompt_rows, c) and c % LANES == 0
    ratio = CONV_TILE // CONV_HALO
    npt = n_prompt_rows // CONV_TILE
    ns = c // LANES
    kern = functools.partial(_conv_prompt_kernel, tiles_per_seq=seq // CONV_TILE, n_prompt_tiles=npt)
    w_strips = w_dw.reshape(CONV_WIDTH, ns, LANES).transpose(1, 0, 2)

    def prompt_tile(i):
        return jnp.minimum(i, npt - 1)

    return pl.pallas_call(
        kern,
        out_shape=jax.ShapeDtypeStruct((t, c), BF16),
        grid=(t // CONV_TILE,),
        in_specs=[pl.BlockSpec((CONV_TILE, c), lambda i: (prompt_tile(i), 0)),
                  pl.BlockSpec((CONV_HALO, c), lambda i: (jnp.maximum(prompt_tile(i) * ratio - 1, 0), 0)),
                  pl.BlockSpec((CONV_TILE, c), lambda i: (jnp.maximum(i - npt, 0), 0)),
                  pl.BlockSpec((ns, CONV_WIDTH, LANES), lambda i: (0, 0, 0)),
                  pl.BlockSpec((ns, 1, LANES), lambda i: (0, 0, 0)),
                  pl.BlockSpec((1, c), lambda i: (0, 0)),
                  pl.BlockSpec((1, c), lambda i: (0, 0))],
        out_specs=pl.BlockSpec((CONV_TILE, c), lambda i: (i, 0)),
        scratch_shapes=[pltpu.VMEM((ns, CONV_HALO + CONV_TILE, LANES), F32),
                        pltpu.VMEM((SUBLANES - 1, CONV_HALO + CONV_TILE - SUBLANES, LANES), F32),
                        pltpu.VMEM((ns, CONV_TILE, LANES), F32)],
        compiler_params=_params("arbitrary"),
        name="conv_prompt",
    )(glu, glu, y_sample, w_strips, b_dw.reshape(ns, 1, LANES), ln_g.reshape(1, c), ln_b.reshape(1, c))


CONV_SAMPLE_TILE = 16


def _conv_sample_kernel(hist_ref, x_ref, w_ref, bdw_ref, g_ref, b_ref, yc_ref, ns_ref):
    nh = CONV_WIDTH - 1
    hist = hist_ref[...]
    x = x_ref[...]
    y = jnp.sum(hist * w_ref[0:nh, :][None], axis=1) + x * w_ref[nh:nh + 1, :] + bdw_ref[...]
    y = _layer_norm(y, g_ref[...], b_ref[...])
    yc_ref[...] = (y * _sigmoid(y)).astype(yc_ref.dtype)
    ns_ref[:, 0:nh - 1, :] = hist_ref[:, 1:nh, :]
    for b in range(CONV_SAMPLE_TILE):
        ns_ref[b, nh - 1:nh, :] = x_ref[b:b + 1, :]


def _conv_sample(hist, glu, w_dw, b_dw, ln_g, ln_b, n_prompt_rows):
    t, c = glu.shape
    db, nh, _ = hist.shape
    tb = CONV_SAMPLE_TILE
    assert nh == CONV_WIDTH - 1 and db % tb == 0 and n_prompt_rows % tb == 0 and t == n_prompt_rows + db
    b0 = n_prompt_rows // tb
    return pl.pallas_call(
        _conv_sample_kernel,
        out_shape=(jax.ShapeDtypeStruct((db, c), BF16),
                   jax.ShapeDtypeStruct((db, nh, c), F32)),
        grid=(db // tb,),
        in_specs=[pl.BlockSpec((tb, nh, c), lambda i: (i, 0, 0)),
                  pl.BlockSpec((tb, c), lambda i: (i + b0, 0)),
                  pl.BlockSpec((CONV_WIDTH, c), lambda i: (0, 0)),
                  pl.BlockSpec((1, c), lambda i: (0, 0)),
                  pl.BlockSpec((1, c), lambda i: (0, 0)),
                  pl.BlockSpec((1, c), lambda i: (0, 0))],
        out_specs=(pl.BlockSpec((tb, c), lambda i: (i, 0)),
                   pl.BlockSpec((tb, nh, c), lambda i: (i, 0, 0))),
        compiler_params=_params("arbitrary"),
        name="conv_sample",
    )(hist, glu, w_dw, b_dw.reshape(1, c), ln_g.reshape(1, c), ln_b.reshape(1, c))


ROUTER_TILE = 320


U32 = jnp.uint32


def _pack_halves(x):
    half = x.shape[1] // 2

    def bf16_bits_high(v):
        u = pltpu.bitcast(v, U32)
        return (u + U32(0x7FFF) + ((u >> 16) & U32(1))) & U32(0xFFFF0000)

    return bf16_bits_high(x[:, half:]) | (bf16_bits_high(x[:, :half]) >> 16)


def _unpack_halves(w):
    return pltpu.bitcast(w << 16, F32), pltpu.bitcast(w & U32(0xFFFF0000), F32)


def _pick_lanes(lane, *cols):
    out = jnp.zeros((cols[0].shape[0], LANES), cols[0].dtype)
    for k, col in enumerate(cols):
        out = jnp.where(lane == k, col, out)
    return out


def _router_kernel(h_ref, g_ref, wr_ref, br_ref, xf_ref, meta_ref, gate_ref, cnt_ref, carry_ref):
    i = pl.program_id(0)
    tm = h_ref.shape[0]

    @pl.when(i == 0)
    def _():
        carry_ref[...] = jnp.zeros_like(carry_ref)

    x = h_ref[...]
    xf = x * lax.rsqrt(jnp.mean(x * x, axis=-1, keepdims=True) + EPS) * g_ref[...]
    xf_ref[...] = _pack_halves(xf)
    logits = jnp.dot(xf, wr_ref[...], preferred_element_type=F32,
                     precision=lax.Precision.HIGHEST) + br_ref[...]
    lane = lax.broadcasted_iota(I32, (tm, LANES), 1)
    lane_f = lane.astype(F32)
    neg_inf = jnp.float32(-jnp.inf)

    def first_lane(hit):
        return jnp.min(jnp.where(hit, lane_f, float(LANES)), axis=-1, keepdims=True).astype(I32)

    is_g = lane < N_EXPERT_GROUPS
    gl = jnp.where(is_g, logits, neg_inf)
    gmax = jnp.max(gl, axis=-1, keepdims=True)
    grp = first_lane(gl == gmax)
    p_grp = 1.0 / jnp.sum(jnp.where(is_g, jnp.exp(logits - gmax), 0.0), axis=-1, keepdims=True)

    lo = EXPERT_LANE0 + grp * EXPERTS_PER_GROUP
    in_grp = (lane >= lo) & (lane < lo + EXPERTS_PER_GROUP)
    el = jnp.where(in_grp, logits, neg_inf)
    v0 = jnp.max(el, axis=-1, keepdims=True)
    j0 = first_lane(el == v0)
    el1 = jnp.where(lane == j0, neg_inf, el)
    v1 = jnp.max(el1, axis=-1, keepdims=True)
    j1 = first_lane(el1 == v1)
    e1 = jnp.exp(v1 - v0)
    gate0 = p_grp * (1.0 / (1.0 + e1))
    gate1 = p_grp * (e1 / (1.0 + e1))

    oh0 = lane == j0
    oh1 = lane == j1
    cnt = jnp.where(oh0 | oh1, 1.0, 0.0)
    r = lax.broadcasted_iota(I32, (tm, tm), 0)
    c = lax.broadcasted_iota(I32, (tm, tm), 1)
    strict_lower = jnp.where(c < r, 1.0, 0.0).astype(BF16)
    before = jnp.dot(strict_lower, cnt.astype(BF16), preferred_element_type=F32) + carry_ref[...]
    rank0 = jnp.sum(jnp.where(oh0, before, 0.0), axis=-1, keepdims=True).astype(I32)
    rank1 = jnp.sum(jnp.where(oh1, before, 0.0), axis=-1, keepdims=True).astype(I32)
    carry_ref[...] = carry_ref[...] + jnp.sum(cnt, axis=0, keepdims=True)
    cnt_ref[...] = carry_ref[...]

    meta_ref[...] = _pick_lanes(lane, j0 - EXPERT_LANE0, j1 - EXPERT_LANE0, rank0, rank1)
    gate_ref[...] = _pick_lanes(lane, gate0, gate1)


def _router(h, g, w_group, b_group, w_expert, b_expert):
    t, d = h.shape
    tm = _largest_tile(t, ROUTER_TILE, SUBLANES)
    pad = LANES - N_EXPERT_GROUPS - N_EXPERTS
    wr = jnp.concatenate([w_group, w_expert, jnp.zeros((d, pad), F32)], axis=1)
    br = jnp.concatenate([b_group, b_expert, jnp.zeros((pad,), F32)]).reshape(1, LANES)
    return pl.pallas_call(
        _router_kernel,
        out_shape=(jax.ShapeDtypeStruct((t, d // 2), U32),
                   jax.ShapeDtypeStruct((t, LANES), I32),
                   jax.ShapeDtypeStruct((t, LANES), F32),
                   jax.ShapeDtypeStruct((1, LANES), F32)),
        grid=(t // tm,),
        in_specs=[pl.BlockSpec((tm, d), lambda i: (i, 0)),
                  pl.BlockSpec((1, d), lambda i: (0, 0)),
                  pl.BlockSpec((d, LANES), lambda i: (0, 0)),
                  pl.BlockSpec((1, LANES), lambda i: (0, 0))],
        out_specs=(pl.BlockSpec((tm, d // 2), lambda i: (i, 0)),
                   pl.BlockSpec((tm, LANES), lambda i: (i, 0)),
                   pl.BlockSpec((tm, LANES), lambda i: (i, 0)),
                   pl.BlockSpec((1, LANES), lambda i: (0, 0))),
        scratch_shapes=[pltpu.VMEM((1, LANES), F32)],
        compiler_params=_params("arbitrary"),
        name="moe_router",
    )(h, g.reshape(1, d), wr, br)


def _row_copy(src, src_row, dst, dst_row, sem):
    return pltpu.make_async_copy(src.at[pl.ds(src_row, 1)], dst.at[pl.ds(dst_row, 1)], sem)


def _block_rows(ref, blk):
    return ref.at[pl.ds(pl.multiple_of(blk * MOE_BLOCK, MOE_BLOCK), MOE_BLOCK)]


def _dispatch_kernel(zflag_ref, xf_ref, meta_ref, pstart_ref, xs_ref, dest_ref,
                     dest_vmem, dest_smem, zbuf, sem, row_sem, zero_sem):
    tm = xf_ref.shape[0]

    @pl.when(pl.program_id(0) == 0)
    def _():
        zbuf[...] = jnp.zeros_like(zbuf)

        def zero_block(blk, carry, wait):
            @pl.when(zflag_ref[blk] != 0)
            def _():
                cp = pltpu.make_async_copy(zbuf, _block_rows(xs_ref, blk), zero_sem)
                if wait:
                    cp.wait()
                else:
                    cp.start()
            return carry

        n_blocks = zflag_ref.shape[0]
        lax.fori_loop(0, n_blocks, functools.partial(zero_block, wait=False), 0)
        lax.fori_loop(0, n_blocks, functools.partial(zero_block, wait=True), 0)

    lane = lax.broadcasted_iota(I32, (tm, LANES), 1)
    meta = meta_ref[...]
    pstart = pstart_ref[...]
    dests = []
    for k in range(TOP_K):
        e = meta[:, k:k + 1]
        rank = meta[:, TOP_K + k:TOP_K + k + 1]
        base = jnp.sum(jnp.where(lane - EXPERT_LANE0 == e, pstart, 0.0), axis=-1, keepdims=True)
        dests.append(base + rank.astype(F32))
    dest_t = jnp.transpose(_pick_lanes(lane, *dests))[0:SUBLANES, :].astype(I32)
    dest_ref[0] = dest_t
    dest_vmem[...] = dest_t
    cp = pltpu.make_async_copy(dest_vmem, dest_smem, sem)
    cp.start()
    cp.wait()

    def issue(tok, carry):
        for k in range(TOP_K):
            _row_copy(xf_ref, tok, xs_ref, dest_smem[k, tok], row_sem).start()
        return carry

    lax.fori_loop(0, tm, issue, 0)

    def drain(tok, carry):
        for k in range(TOP_K):
            _row_copy(xf_ref, 0, xs_ref, 0, row_sem).wait()
        return carry

    lax.fori_loop(0, tm, drain, 0)


def _dispatch(xf, meta, pstart_lanes, zero_flags):
    t, d = xf.shape
    tm = CHUNK
    nt = t // tm
    n_rows = zero_flags.shape[0] * MOE_BLOCK
    grid_spec = pltpu.PrefetchScalarGridSpec(
        num_scalar_prefetch=1,
        grid=(nt,),
        in_specs=[pl.BlockSpec((tm, d), lambda i, zf: (i, 0)),
                  pl.BlockSpec((tm, LANES), lambda i, zf: (i, 0)),
                  pl.BlockSpec((1, LANES), lambda i, zf: (0, 0))],
        out_specs=(pl.BlockSpec(memory_space=pl.ANY),
                   pl.BlockSpec((1, SUBLANES, LANES), lambda i, zf: (i, 0, 0))),
        scratch_shapes=[pltpu.VMEM((SUBLANES, LANES), I32),
                        pltpu.SMEM((SUBLANES, LANES), I32),
                        pltpu.VMEM((MOE_BLOCK, d), xf.dtype),
                        pltpu.SemaphoreType.DMA,
                        pltpu.SemaphoreType.DMA,
                        pltpu.SemaphoreType.DMA])
    return pl.pallas_call(
        _dispatch_kernel,
        out_shape=(jax.ShapeDtypeStruct((n_rows, d), xf.dtype),
                   jax.ShapeDtypeStruct((nt, SUBLANES, LANES), I32)),
        grid_spec=grid_spec,
        compiler_params=_params("arbitrary"),
        name="moe_dispatch",
    )(zero_flags, xf, meta, pstart_lanes)


FFN_DOWN_COLS = 1024


def _ffn_kernel(pe_ref, pb_ref, pn_ref, pz_ref, xs_ref, wg_ref, wu_ref, wd_ref, ys_ref,
                xbuf, acc, ypk, zbuf, in_sem, out_sem, zero_sem, *, n_ftiles, n_pass):
    del pe_ref
    p = pl.program_id(0)
    j = pl.program_id(1)
    d = acc.shape[1]
    half = d // 2
    last_tile = j == n_ftiles - 1
    nblk = pn_ref[p]
    slot = lax.rem(p, 2)
    p_next = jnp.minimum(p + 1, n_pass - 1)
    nblk_next = jnp.where(p + 1 < n_pass, pn_ref[p_next], 0)
    p_prev = jnp.maximum(p - 1, 0)
    nblk_prev = jnp.where(p > 0, pn_ref[p_prev], 0)

    def for_blocks(count, fn):
        for s in range(PASS_BLOCKS):
            pl.when(s < count)(functools.partial(fn, s))

    def block_slice(s):
        return pl.ds(s * MOE_BLOCK, MOE_BLOCK)

    def x_copy(q, s):
        q_slot = lax.rem(q, 2)
        return pltpu.make_async_copy(_block_rows(xs_ref, pb_ref[q] + s),
                                     xbuf.at[q_slot, block_slice(s)], in_sem.at[q_slot])

    def y_copy(q, s):
        return pltpu.make_async_copy(ypk.at[block_slice(s)], _block_rows(ys_ref, pb_ref[q] + s), out_sem)

    def wait_prev_writeback():
        for_blocks(nblk_prev, lambda s: y_copy(p_prev, s).wait())

    @pl.when((p == 0) & (j == 0))
    def _():
        zbuf[...] = jnp.zeros_like(zbuf)
        acc[...] = jnp.zeros_like(acc)
        for_blocks(nblk, lambda s: x_copy(p, s).start())

    @pl.when(j == 0)
    def _():
        def zero_copy(s):
            return pltpu.make_async_copy(zbuf, _block_rows(ys_ref, pb_ref[p] + s), zero_sem)

        for_blocks(pz_ref[p], lambda s: zero_copy(s).start())
        for_blocks(pz_ref[p], lambda s: zero_copy(s).wait())
        for_blocks(nblk, lambda s: x_copy(p, s).wait())
        for_blocks(nblk_next, lambda s: x_copy(p_next, s).start())

    def compute(n):
        rows = n * MOE_BLOCK
        x_lo, x_hi = _unpack_halves(xbuf[slot, 0:rows, :])
        x_lo = x_lo.astype(BF16)
        x_hi = x_hi.astype(BF16)

        def in_proj(w_ref):
            return (jnp.dot(x_lo, w_ref[0:half, :].astype(BF16), preferred_element_type=F32)
                    + jnp.dot(x_hi, w_ref[half:d, :].astype(BF16), preferred_element_type=F32))

        gate = in_proj(wg_ref)
        up = in_proj(wu_ref)
        hmid = (gate * _sigmoid(gate) * up).astype(BF16)
        down_cols = min(FFN_DOWN_COLS, d)
        for c0 in range(0, d, down_cols):
            cols = slice(c0, c0 + down_cols)
            y = jnp.dot(hmid, wd_ref[:, cols].astype(BF16), preferred_element_type=F32)
            acc[0:rows, cols] = y + jnp.where(j == 0, 0.0, acc[0:rows, cols])

        @pl.when(last_tile)
        def _():
            wait_prev_writeback()
            ypk[0:rows, :] = _pack_halves(acc[0:rows, :])

    for n in range(1, PASS_BLOCKS + 1):
        pl.when(nblk == n)(functools.partial(compute, n))

    pl.when(last_tile & (nblk == 0))(wait_prev_writeback)

    @pl.when(last_tile)
    def _():
        for_blocks(nblk, lambda s: y_copy(p, s).start())

        @pl.when(p == n_pass - 1)
        def _():
            for_blocks(nblk, lambda s: y_copy(p, s).wait())


def _ffn(xs, w_gate, w_up, w_down, layer, pass_expert, pass_block, pass_nblk, pass_nzero):
    n_rows, half = xs.shape
    d = 2 * half
    de = w_gate.shape[-1]
    assert d % min(FFN_DOWN_COLS, d) == 0 and w_gate.shape[-2] == d
    tf = _largest_tile(de, 256, LANES)
    n_ftiles = de // tf
    n_pass = pass_expert.shape[0]
    last = n_ftiles - 1

    def f_idx(p, j, pn):
        return jnp.where(pn[p] > 0, j, last)

    def in_proj(p, j, pe, pb, pn, pz):
        return (layer, pe[p], 0, f_idx(p, j, pn))

    def out_proj(p, j, pe, pb, pn, pz):
        return (layer, pe[p], f_idx(p, j, pn), 0)

    grid_spec = pltpu.PrefetchScalarGridSpec(
        num_scalar_prefetch=4,
        grid=(n_pass, n_ftiles),
        in_specs=[pl.BlockSpec(memory_space=pl.ANY),
                  pl.BlockSpec((None, None, d, tf), in_proj),
                  pl.BlockSpec((None, None, d, tf), in_proj),
                  pl.BlockSpec((None, None, tf, d), out_proj)],
        out_specs=pl.BlockSpec(memory_space=pl.ANY),
        scratch_shapes=[pltpu.VMEM((2, PASS_BLOCKS * MOE_BLOCK, half), U32),
                        pltpu.VMEM((PASS_BLOCKS * MOE_BLOCK, d), F32),
                        pltpu.VMEM((PASS_BLOCKS * MOE_BLOCK, half), U32),
                        pltpu.VMEM((MOE_BLOCK, half), U32),
                        pltpu.SemaphoreType.DMA((2,)),
                        pltpu.SemaphoreType.DMA,
                        pltpu.SemaphoreType.DMA])
    return pl.pallas_call(
        functools.partial(_ffn_kernel, n_ftiles=n_ftiles, n_pass=n_pass),
        out_shape=jax.ShapeDtypeStruct((n_rows, half), U32),
        grid_spec=grid_spec,
        compiler_params=_params("arbitrary", "arbitrary"),
        name="moe_ffn",
    )(pass_expert, pass_block, pass_nblk, pass_nzero, xs, w_gate, w_up, w_down)


def _combine_kernel(h_ref, gate_ref, dest_ref, g_ref, ys_ref, out_a, out_b, ybuf, dest_smem, sem, row_sem,
                    *, final, n_prompt_tiles):
    tm = h_ref.shape[0]
    cp = pltpu.make_async_copy(dest_ref.at[0], dest_smem, sem)
    cp.start()
    cp.wait()

    def issue(tok, carry):
        for k in range(TOP_K):
            _row_copy(ys_ref, dest_smem[k, tok], ybuf.at[k], tok, row_sem).start()
        return carry

    lax.fori_loop(0, tm, issue, 0)

    def drain(tok, carry):
        for k in range(TOP_K):
            _row_copy(ys_ref, 0, ybuf.at[k], 0, row_sem).wait()
        return carry

    lax.fori_loop(0, tm, drain, 0)

    gates = gate_ref[...]
    f_lo, f_hi = None, None
    for k in range(TOP_K):
        y_lo, y_hi = _unpack_halves(ybuf[k])
        gate = gates[:, k:k + 1]
        f_lo = y_lo * gate if f_lo is None else f_lo + y_lo * gate
        f_hi = y_hi * gate if f_hi is None else f_hi + y_hi * gate
    hn = h_ref[...] + jnp.concatenate([f_lo, f_hi], axis=1)
    normed = hn * lax.rsqrt(jnp.mean(hn * hn, axis=-1, keepdims=True) + EPS) * g_ref[...]
    if final:
        i = pl.program_id(0)

        @pl.when(i < n_prompt_tiles)
        def _():
            out_a[...] = normed

        @pl.when(i >= n_prompt_tiles)
        def _():
            out_b[...] = normed
    else:
        out_a[...] = hn
        out_b[...] = normed.astype(out_b.dtype)


def _combine(h, gates, dest, ys, next_g, final, n_prompt_rows):
    t, d = h.shape
    tm = CHUNK
    assert n_prompt_rows % tm == 0 and t % tm == 0
    npt = n_prompt_rows // tm
    if final:
        out_shape = (jax.ShapeDtypeStruct((n_prompt_rows, d), F32),
                     jax.ShapeDtypeStruct((t - n_prompt_rows, d), F32))
        out_specs = (pl.BlockSpec((tm, d), lambda i: (jnp.minimum(i, npt - 1), 0)),
                     pl.BlockSpec((tm, d), lambda i: (jnp.maximum(i - npt, 0), 0)))
    else:
        out_shape = (jax.ShapeDtypeStruct((t, d), F32), jax.ShapeDtypeStruct((t, d), BF16))
        out_specs = (pl.BlockSpec((tm, d), lambda i: (i, 0)), pl.BlockSpec((tm, d), lambda i: (i, 0)))
    return pl.pallas_call(
        functools.partial(_combine_kernel, final=final, n_prompt_tiles=npt),
        out_shape=out_shape,
        grid=(t // tm,),
        in_specs=[pl.BlockSpec((tm, d), lambda i: (i, 0)),
                  pl.BlockSpec((tm, LANES), lambda i: (i, 0)),
                  pl.BlockSpec((1, SUBLANES, LANES), lambda i: (i, 0, 0)),
                  pl.BlockSpec((1, d), lambda i: (0, 0)),
                  pl.BlockSpec(memory_space=pl.ANY)],
        out_specs=out_specs,
        scratch_shapes=[pltpu.VMEM((TOP_K, tm, d // 2), U32),
                        pltpu.SMEM((SUBLANES, LANES), I32),
                        pltpu.SemaphoreType.DMA,
                        pltpu.SemaphoreType.DMA],
        compiler_params=_params("arbitrary"),
        name="moe_combine",
    )(h, gates, dest, next_g.reshape(1, d), ys)


def _pass_tables(counts, n_blocks, n_pass):
    blocks = (counts + MOE_BLOCK - 1) // MOE_BLOCK
    block_end = jnp.cumsum(blocks)
    block_start = block_end - blocks
    used = block_end[-1]
    passes = (blocks + PASS_BLOCKS - 1) // PASS_BLOCKS
    pass_end = jnp.cumsum(passes)
    pass_start = pass_end - passes
    total = pass_end[-1]
    p = jnp.arange(n_pass, dtype=I32)
    valid = p < total
    pc = jnp.where(valid, p, jnp.maximum(total - 1, 0))
    e = jnp.minimum(jnp.searchsorted(pass_end, pc, side="right"), N_EXPERTS - 1).astype(I32)
    k = pc - pass_start[e]
    tail_first = used + (p - total) * PASS_BLOCKS
    first = jnp.where(valid, block_start[e] + k * PASS_BLOCKS, tail_first)
    nblk = jnp.where(valid, jnp.clip(blocks[e] - k * PASS_BLOCKS, 0, PASS_BLOCKS), 0)
    nzero = jnp.where(valid, 0, jnp.clip(n_blocks - tail_first, 0, PASS_BLOCKS))
    blk = jnp.arange(n_blocks, dtype=I32)
    ragged_last = jnp.where(counts % MOE_BLOCK != 0, block_end - 1, n_blocks)
    zero_flags = (blk >= used) | jnp.zeros((n_blocks,), bool).at[ragged_last].set(True, mode="drop")
    return (e, first.astype(I32), nblk.astype(I32), nzero.astype(I32),
            (block_start * MOE_BLOCK).astype(I32), zero_flags.astype(I32))


def _hier_moe(h, norm_g, w_group, b_group, w_expert, b_expert, w_gate, w_up, w_down, layer,
              next_g, final, n_prompt_rows):
    t, d = h.shape
    n_assign = t * TOP_K
    n_blocks = -(-(n_assign + N_EXPERTS * (MOE_BLOCK - 1)) // MOE_BLOCK)
    n_pass = (n_blocks + N_EXPERTS * (PASS_BLOCKS - 1)) // PASS_BLOCKS + 2
    xf, meta, gates, cnt = _router(h, norm_g, w_group, b_group, w_expert, b_expert)
    counts = cnt[0, EXPERT_LANE0:EXPERT_LANE0 + N_EXPERTS].astype(I32)
    pass_expert, pass_block, pass_nblk, pass_nzero, row_start, zero_flags = _pass_tables(
        counts, n_blocks, n_pass)
    pstart_lanes = jnp.zeros((1, LANES), F32).at[0, EXPERT_LANE0:EXPERT_LANE0 + N_EXPERTS].set(
        row_start.astype(F32))
    xs, dest = _dispatch(xf, meta, pstart_lanes, zero_flags)
    ys = _ffn(xs, w_gate, w_up, w_down, layer, pass_expert, pass_block, pass_nblk, pass_nzero)
    return _combine(h, gates, dest, ys, next_g, final, n_prompt_rows)


def _embed_kernel(xp_ref, xs_ref, g_ref, h_ref, n_ref, *, n_prompt_tiles):
    i = pl.program_id(0)

    def emit(x):
        h_ref[...] = x
        y = x * lax.rsqrt(jnp.mean(x * x, axis=-1, keepdims=True) + EPS)
        n_ref[...] = (y * g_ref[...]).astype(n_ref.dtype)

    @pl.when(i < n_prompt_tiles)
    def _():
        emit(xp_ref[...])

    @pl.when(i >= n_prompt_tiles)
    def _():
        emit(xs_ref[...])


def _embed(x_prompt, x_sample, g):
    n_p, d = x_prompt.shape
    n_s = x_sample.shape[0]
    tm = CHUNK
    assert n_p % tm == 0 and n_s % tm == 0
    npt = n_p // tm
    t = n_p + n_s
    return pl.pallas_call(
        functools.partial(_embed_kernel, n_prompt_tiles=npt),
        out_shape=(jax.ShapeDtypeStruct((t, d), F32), jax.ShapeDtypeStruct((t, d), BF16)),
        grid=(t // tm,),
        in_specs=[pl.BlockSpec((tm, d), lambda i: (jnp.minimum(i, npt - 1), 0)),
                  pl.BlockSpec((tm, d), lambda i: (jnp.maximum(i - npt, 0), 0)),
                  pl.BlockSpec((1, d), lambda i: (0, 0))],
        out_specs=(pl.BlockSpec((tm, d), lambda i: (i, 0)), pl.BlockSpec((tm, d), lambda i: (i, 0))),
        compiler_params=_params("arbitrary"),
        name="embed_norm",
    )(x_prompt, x_sample, g.reshape(1, d))


def kernel(x_prompt, x_sample, state_conv, mix_norm_g, sgu_w_in, sgu_b_in, sgu_v_norm_g, sgu_v_norm_b, sgu_w_s, sgu_b_s, sgu_w_out, sgu_b_out, conv_w_pw1, conv_b_pw1, conv_w_dw, conv_b_dw, conv_ln_g, conv_ln_b, conv_w_pw2, conv_b_pw2, ffn_norm_g, moe_w_group, moe_b_group, moe_w_expert, moe_b_expert, moe_w_gate, moe_w_up, moe_w_down, final_norm_g):
    batch, seq, d = x_prompt.shape
    dec_batch, dec_seq, _ = x_sample.shape
    assert dec_seq == 1
    n_p = batch * seq
    depth = mix_norm_g.shape[0]
    h, hn = _embed(x_prompt.reshape(n_p, d), x_sample.reshape(dec_batch, d), mix_norm_g[0])

    sgu_v_new, conv_p_new, conv_s_new = [], [], []
    for i in range(depth):
        j = i // 2
        final = i == depth - 1
        if i % 2 == 0:
            z = _mm_gelu(hn, _cast_bf16(sgu_w_in[j]), sgu_b_in[j])
            us, v_s = _sgu_mid(z, sgu_v_norm_g[j], sgu_v_norm_b[j], sgu_w_s[j], sgu_b_s[j], n_p)
            h = _mm_res(us, _cast_bf16(sgu_w_out[j]), sgu_b_out[j], h)
            sgu_v_new.append(v_s.reshape(dec_batch, dec_seq, -1))
        else:
            glu = _mm_glu(hn, _cast_bf16(conv_w_pw1[j]), conv_b_pw1[j])
            conv_args = (conv_w_dw[j], conv_b_dw[j], conv_ln_g[j], conv_ln_b[j])
            y_s, c_s = _conv_sample(state_conv[j], glu, *conv_args, n_p)
            yc = _conv_prompt(glu, y_s, *conv_args, n_p, seq)
            h = _mm_res(yc, _cast_bf16(conv_w_pw2[j]), conv_b_pw2[j], h)
            tail = CONV_WIDTH - 1
            conv_p_new.append(jnp.stack([glu[(b + 1) * seq - tail:(b + 1) * seq] for b in range(batch)]))
            conv_s_new.append(c_s)
        next_g = final_norm_g if final else mix_norm_g[i + 1]
        h, hn = _hier_moe(h, ffn_norm_g[i], moe_w_group[i], moe_b_group[i], moe_w_expert[i],
                          moe_b_expert[i], moe_w_gate, moe_w_up, moe_w_down, i, next_g, final, n_p)

    y_prompt = h.reshape(batch, seq, d)
    y_sample = hn.reshape(dec_batch, dec_seq, d)
    def stack(parts):
        return parts[0][None] if len(parts) == 1 else jnp.stack(parts)

    return (y_prompt, y_sample, stack(sgu_v_new), stack(conv_p_new), stack(conv_s_new))
```

```python
import functools

import jax
import jax.numpy as jnp
from jax import lax
from jax.experimental import pallas as pl
from jax.experimental.pallas import tpu as pltpu

F32 = jnp.float32
BF16 = jnp.bfloat16
I32 = jnp.int32

EPS = 1e-6
CHUNK = 128
LANES = 128
SUBLANES = 8
CONV_WIDTH = 31
N_EXPERT_GROUPS = 8
EXPERTS_PER_GROUP = 8
N_EXPERTS = N_EXPERT_GROUPS * EXPERTS_PER_GROUP
TOP_K = 2
MOE_BLOCK = 128
PASS_BLOCKS = 3
EXPERT_LANE0 = N_EXPERT_GROUPS
VMEM_LIMIT_BYTES = 56 * 1024 * 1024


def _params(*sem):
    return pltpu.CompilerParams(dimension_semantics=sem, vmem_limit_bytes=VMEM_LIMIT_BYTES)


def _largest_tile(n, cap, mult):
    t = (min(cap, n) // mult) * mult
    while t > mult and n % t:
        t -= mult
    assert t >= mult and n % t == 0, (n, cap, mult)
    return t


def _sigmoid(x):
    return 1.0 / (1.0 + jnp.exp(-x))


def _stage_bf16(w_ref, wb_ref):
    @pl.when(pl.program_id(1) == 0)
    def _():
        wb_ref[...] = w_ref[...].astype(BF16)


def _mm_gelu_kernel(x_ref, w_ref, b_ref, o_ref, wb_ref):
    _stage_bf16(w_ref, wb_ref)
    acc = jnp.dot(x_ref[...], wb_ref[...], preferred_element_type=F32) + b_ref[...]
    o_ref[...] = jax.nn.gelu(acc).astype(o_ref.dtype)


def _mm_glu_kernel(x_ref, wa_ref, wg_ref, ba_ref, bg_ref, o_ref, wab_ref, wgb_ref):
    _stage_bf16(wa_ref, wab_ref)
    _stage_bf16(wg_ref, wgb_ref)
    x = x_ref[...]
    a = jnp.dot(x, wab_ref[...], preferred_element_type=F32) + ba_ref[...]
    gate = jnp.dot(x, wgb_ref[...], preferred_element_type=F32) + bg_ref[...]
    o_ref[...] = (a * _sigmoid(gate)).astype(o_ref.dtype)


def _mm_res_kernel(x_ref, w_ref, b_ref, r_ref, o_ref, wb_ref):
    _stage_bf16(w_ref, wb_ref)
    acc = jnp.dot(x_ref[...], wb_ref[...], preferred_element_type=F32) + b_ref[...]
    o_ref[...] = (r_ref[...] + acc).astype(o_ref.dtype)


MM_ROW_TILE = 832
MM_COL_TILE = 512


def _mm_call(kern, name, x, weights, col_blocks, biases, res, tn, n_out):
    t, k = x.shape
    tm = _largest_tile(t, MM_ROW_TILE, 16)
    in_specs = [pl.BlockSpec((tm, k), lambda j, i: (i, 0))]
    in_specs += [pl.BlockSpec((k, tn), functools.partial(lambda j, i, cb: (0, cb(j)), cb=cb)) for cb in col_blocks]
    in_specs += [pl.BlockSpec((1, tn), functools.partial(lambda j, i, cb: (0, cb(j)), cb=cb)) for cb in col_blocks]
    args = [x, *weights, *biases]
    if res is not None:
        in_specs.append(pl.BlockSpec((tm, tn), lambda j, i: (i, j)))
        args.append(res)
    return pl.pallas_call(
        kern,
        out_shape=jax.ShapeDtypeStruct((t, n_out), F32),
        grid=(n_out // tn, t // tm),
        in_specs=in_specs,
        out_specs=pl.BlockSpec((tm, tn), lambda j, i: (i, j)),
        scratch_shapes=[pltpu.VMEM((k, tn), BF16) for _ in weights],
        compiler_params=_params("arbitrary", "arbitrary"),
        name=name,
    )(*args)


def _mm_gelu(x, w, b):
    n = w.shape[1]
    tn = _largest_tile(n, MM_COL_TILE, LANES)
    return _mm_call(_mm_gelu_kernel, "mm_gelu", x, [w], [lambda j: j], [b.reshape(1, n)], None, tn, n)


def _mm_glu(x, w, b):
    n = w.shape[1] // 2
    tn = _largest_tile(n, MM_COL_TILE // 2, LANES)
    nb = n // tn
    b2 = b.reshape(1, 2 * n)
    return _mm_call(_mm_glu_kernel, "mm_glu", x, [w, w], [lambda j: j, lambda j: j + nb], [b2, b2], None, tn, n)


def _mm_res(x, w, b, res):
    n = w.shape[1]
    tn = _largest_tile(n, MM_COL_TILE, LANES)
    return _mm_call(_mm_res_kernel, "mm_res", x, [w], [lambda j: j], [b.reshape(1, n)], res, tn, n)


def _layer_norm(x, g, b):
    xc = x - jnp.mean(x, axis=-1, keepdims=True)
    y = xc * lax.rsqrt(jnp.mean(xc * xc, axis=-1, keepdims=True) + EPS)
    return y * g + b


def _sgu_mid_kernel(u_ref, v_ref, g_ref, b_ref, ws_ref, bias_ref, w00_ref, b00_ref,
                    us_ref, vs_ref, wt_ref, *, n_groups, n_prompt_tiles):
    i = pl.program_id(0)

    @pl.when(i == 0)
    def _():
        r = lax.broadcasted_iota(I32, (CHUNK, CHUNK), 0)
        c = lax.broadcasted_iota(I32, (CHUNK, CHUNK), 1)
        tril = (c <= r).astype(F32)
        for g in range(n_groups):
            wt_ref[g] = (ws_ref[g] * tril).astype(BF16)

    vn = _layer_norm(v_ref[...], g_ref[...], b_ref[...])

    @pl.when(i < n_prompt_tiles)
    def _():
        vb = vn.astype(BF16)
        for g in range(n_groups):
            sl = slice(g * LANES, (g + 1) * LANES)
            s = jnp.dot(wt_ref[g], vb[:, sl], preferred_element_type=F32) + bias_ref[:, sl]
            us_ref[:, sl] = (u_ref[:, sl] * s).astype(us_ref.dtype)

    @pl.when(i >= n_prompt_tiles)
    def _():
        s = vn * w00_ref[...] + b00_ref[...]
        us_ref[...] = (u_ref[...] * s).astype(us_ref.dtype)
        vs_ref[...] = vn


def _sgu_mid(z, v_g, v_b, w_s, b_s, n_prompt_rows):
    t, w2 = z.shape
    w = w2 // 2
    n_groups = w // LANES
    assert w_s.shape == (n_groups, CHUNK, CHUNK) and t % CHUNK == 0 and n_prompt_rows % CHUNK == 0
    npt = n_prompt_rows // CHUNK
    nst = t // CHUNK - npt
    bias_full = jnp.repeat(b_s.T, LANES, axis=1)
    w00 = jnp.repeat(w_s[:, 0, 0], LANES).reshape(1, w)
    b00 = jnp.repeat(b_s[:, 0], LANES).reshape(1, w)
    kern = functools.partial(_sgu_mid_kernel, n_groups=n_groups, n_prompt_tiles=npt)
    return pl.pallas_call(
        kern,
        out_shape=(jax.ShapeDtypeStruct((t, w), BF16),
                   jax.ShapeDtypeStruct((nst * CHUNK, w), F32)),
        grid=(t // CHUNK,),
        in_specs=[pl.BlockSpec((CHUNK, w), lambda i: (i, 0)),
                  pl.BlockSpec((CHUNK, w), lambda i: (i, 1)),
                  pl.BlockSpec((1, w), lambda i: (0, 0)),
                  pl.BlockSpec((1, w), lambda i: (0, 0)),
                  pl.BlockSpec((n_groups, CHUNK, CHUNK), lambda i: (0, 0, 0)),
                  pl.BlockSpec((CHUNK, w), lambda i: (0, 0)),
                  pl.BlockSpec((1, w), lambda i: (0, 0)),
                  pl.BlockSpec((1, w), lambda i: (0, 0))],
        out_specs=(pl.BlockSpec((CHUNK, w), lambda i: (i, 0)),
                   pl.BlockSpec((CHUNK, w), lambda i: (jnp.maximum(i - npt, 0), 0))),
        scratch_shapes=[pltpu.VMEM((n_groups, CHUNK, CHUNK), BF16)],
        compiler_params=_params("arbitrary"),
        name="sgu_mid",
    )(z, z, v_g.reshape(1, w), v_b.reshape(1, w), w_s, bias_full, w00, b00)


CONV_HALO = 32
CONV_TILE = 128


def _conv_prompt_kernel(x_ref, halo_ref, ysamp_ref, w_ref, bdw_ref, g_ref, b_ref, o_ref,
                        ext_ref, sh_ref, y_ref, *, tiles_per_seq, n_prompt_tiles):
    i = pl.program_id(0)
    n_strips = ext_ref.shape[0]
    ext_rows = CONV_HALO + CONV_TILE
    sh_rows = sh_ref.shape[1]
    first = CONV_HALO - (CONV_WIDTH - 1)

    @pl.when(i >= n_prompt_tiles)
    def _():
        o_ref[...] = ysamp_ref[...]

    @pl.when(i < n_prompt_tiles)
    def _():
        @pl.when(i % tiles_per_seq == 0)
        def _():
            ext_ref[:, 0:CONV_HALO, :] = jnp.zeros((n_strips, CONV_HALO, LANES), F32)

        @pl.when(i % tiles_per_seq != 0)
        def _():
            for s in range(n_strips):
                ext_ref[s, 0:CONV_HALO, :] = halo_ref[:, s * LANES:(s + 1) * LANES]

        for s in range(n_strips):
            ext_ref[s, CONV_HALO:ext_rows, :] = x_ref[:, s * LANES:(s + 1) * LANES]

        def strip(s, carry):
            for r in range(1, SUBLANES):
                sh_ref[r - 1] = ext_ref[s, r:r + sh_rows, :]
            taps_w = [jnp.broadcast_to(w_ref[s, k:k + 1, :], (SUBLANES, LANES)) for k in range(CONV_WIDTH)]
            bias = jnp.broadcast_to(bdw_ref[s], (SUBLANES, LANES))
            for rb in range(CONV_TILE // SUBLANES):
                acc = bias
                for k in range(CONV_WIDTH):
                    m, r = divmod(first + k, SUBLANES)
                    rows = slice((rb + m) * SUBLANES, (rb + m + 1) * SUBLANES)
                    tap = ext_ref[s, rows, :] if r == 0 else sh_ref[r - 1, rows, :]
                    acc = acc + tap * taps_w[k]
                y_ref[s, rb * SUBLANES:(rb + 1) * SUBLANES, :] = acc
            return carry

        lax.fori_loop(0, n_strips, strip, 0)

        inv_c = 1.0 / (n_strips * LANES)
        total = jnp.zeros((CONV_TILE, 1), F32)
        for s in range(n_strips):
            total = total + jnp.sum(y_ref[s], axis=-1, keepdims=True)
        mean = total * inv_c
        sq = jnp.zeros((CONV_TILE, 1), F32)
        for s in range(n_strips):
            yc = y_ref[s] - mean
            sq = sq + jnp.sum(yc * yc, axis=-1, keepdims=True)
        rstd = lax.rsqrt(sq * inv_c + EPS)
        for s in range(n_strips):
            cols = slice(s * LANES, (s + 1) * LANES)
            y = (y_ref[s] - mean) * rstd * g_ref[:, cols] + b_ref[:, cols]
            o_ref[:, cols] = (y * _sigmoid(y)).astype(o_ref.dtype)


def _conv_prompt(glu, y_sample, w_dw, b_dw, ln_g, ln_b, n_prompt_rows, seq):
    t, c = glu.shape
    assert seq % CONV_TILE == 0 and n_prompt_rows % seq == 0 and t % CONV_TILE == 0
    assert y_sample.shape == (t - n_prompt_rows, c) and c % LANES == 0
    ratio = CONV_TILE // CONV_HALO
    npt = n_prompt_rows // CONV_TILE
    ns = c // LANES
    kern = functools.partial(_conv_prompt_kernel, tiles_per_seq=seq // CONV_TILE, n_prompt_tiles=npt)
    w_strips = w_dw.reshape(CONV_WIDTH, ns, LANES).transpose(1, 0, 2)

    def prompt_tile(i):
        return jnp.minimum(i, npt - 1)

    return pl.pallas_call(
        kern,
        out_shape=jax.ShapeDtypeStruct((t, c), BF16),
        grid=(t // CONV_TILE,),
        in_specs=[pl.BlockSpec((CONV_TILE, c), lambda i: (prompt_tile(i), 0)),
                  pl.BlockSpec((CONV_HALO, c), lambda i: (jnp.maximum(prompt_tile(i) * ratio - 1, 0), 0)),
                  pl.BlockSpec((CONV_TILE, c), lambda i: (jnp.maximum(i - npt, 0), 0)),
                  pl.BlockSpec((ns, CONV_WIDTH, LANES), lambda i: (0, 0, 0)),
                  pl.BlockSpec((ns, 1, LANES), lambda i: (0, 0, 0)),
                  pl.BlockSpec((1, c), lambda i: (0, 0)),
                  pl.BlockSpec((1, c), lambda i: (0, 0))],
        out_specs=pl.BlockSpec((CONV_TILE, c), lambda i: (i, 0)),
        scratch_shapes=[pltpu.VMEM((ns, CONV_HALO + CONV_TILE, LANES), F32),
                        pltpu.VMEM((SUBLANES - 1, CONV_HALO + CONV_TILE - SUBLANES, LANES), F32),
                        pltpu.VMEM((ns, CONV_TILE, LANES), F32)],
        compiler_params=_params("arbitrary"),
        name="conv_prompt",
    )(glu, glu, y_sample, w_strips, b_dw.reshape(ns, 1, LANES), ln_g.reshape(1, c), ln_b.reshape(1, c))


CONV_SAMPLE_TILE = 16


def _conv_sample_kernel(hist_ref, x_ref, w_ref, bdw_ref, g_ref, b_ref, yc_ref, ns_ref):
    nh = CONV_WIDTH - 1
    x = x_ref[...]
    acc = x * w_ref[nh:nh + 1, :] + bdw_ref[...]
    for k in range(nh):
        acc = acc + hist_ref[k] * w_ref[k:k + 1, :]
    y = _layer_norm(acc, g_ref[...], b_ref[...])
    yc_ref[...] = (y * _sigmoid(y)).astype(yc_ref.dtype)
    for k in range(nh - 1):
        ns_ref[k] = hist_ref[k + 1]
    ns_ref[nh - 1] = x


def _conv_sample(hist, glu, w_dw, b_dw, ln_g, ln_b, n_prompt_rows):
    t, c = glu.shape
    db, nh, _ = hist.shape
    tb = CONV_SAMPLE_TILE
    assert nh == CONV_WIDTH - 1 and db % tb == 0 and n_prompt_rows % tb == 0 and t == n_prompt_rows + db
    b0 = n_prompt_rows // tb
    hist_t = jnp.transpose(hist, (1, 0, 2))
    yc, ns_t = pl.pallas_call(
        _conv_sample_kernel,
        out_shape=(jax.ShapeDtypeStruct((db, c), BF16),
                   jax.ShapeDtypeStruct((nh, db, c), F32)),
        grid=(db // tb,),
        in_specs=[pl.BlockSpec((nh, tb, c), lambda i: (0, i, 0)),
                  pl.BlockSpec((tb, c), lambda i: (i + b0, 0)),
                  pl.BlockSpec((CONV_WIDTH, c), lambda i: (0, 0)),
                  pl.BlockSpec((1, c), lambda i: (0, 0)),
                  pl.BlockSpec((1, c), lambda i: (0, 0)),
                  pl.BlockSpec((1, c), lambda i: (0, 0))],
        out_specs=(pl.BlockSpec((tb, c), lambda i: (i, 0)),
                   pl.BlockSpec((nh, tb, c), lambda i: (0, i, 0))),
        compiler_params=_params("arbitrary"),
        name="conv_sample",
    )(hist_t, glu, w_dw, b_dw.reshape(1, c), ln_g.reshape(1, c), ln_b.reshape(1, c))
    return yc, jnp.transpose(ns_t, (1, 0, 2))


ROUTER_TILE = 320


U32 = jnp.uint32


def _pack_halves(x):
    half = x.shape[1] // 2

    def bf16_bits_high(v):
        u = pltpu.bitcast(v, U32)
        return (u + U32(0x7FFF) + ((u >> 16) & U32(1))) & U32(0xFFFF0000)

    return bf16_bits_high(x[:, half:]) | (bf16_bits_high(x[:, :half]) >> 16)


def _unpack_halves(w):
    return pltpu.bitcast(w << 16, F32), pltpu.bitcast(w & U32(0xFFFF0000), F32)


def _pick_lanes(lane, *cols):
    out = jnp.zeros((cols[0].shape[0], LANES), cols[0].dtype)
    for k, col in enumerate(cols):
        out = jnp.where(lane == k, col, out)
    return out


def _router_kernel(h_ref, g_ref, wr_ref, br_ref, xf_ref, meta_ref, gate_ref, cnt_ref, carry_ref):
    i = pl.program_id(0)
    tm = h_ref.shape[0]

    @pl.when(i == 0)
    def _():
        carry_ref[...] = jnp.zeros_like(carry_ref)

    x = h_ref[...]
    xf = x * lax.rsqrt(jnp.mean(x * x, axis=-1, keepdims=True) + EPS) * g_ref[...]
    xf_ref[...] = _pack_halves(xf)
    x_hi = xf.astype(BF16)
    x_lo = (xf - x_hi.astype(F32)).astype(BF16)
    first_order = jnp.dot(x_hi, wr_ref[...], preferred_element_type=F32)
    logits = (first_order[:, 0:LANES] + first_order[:, LANES:2 * LANES]
              + jnp.dot(x_lo, wr_ref[:, 0:LANES], preferred_element_type=F32) + br_ref[...])
    lane = lax.broadcasted_iota(I32, (tm, LANES), 1)
    lane_f = lane.astype(F32)
    neg_inf = jnp.float32(-jnp.inf)

    def first_lane(hit):
        return jnp.min(jnp.where(hit, lane_f, float(LANES)), axis=-1, keepdims=True).astype(I32)

    is_g = lane < N_EXPERT_GROUPS
    gl = jnp.where(is_g, logits, neg_inf)
    gmax = jnp.max(gl, axis=-1, keepdims=True)
    grp = first_lane(gl == gmax)
    p_grp = 1.0 / jnp.sum(jnp.where(is_g, jnp.exp(logits - gmax), 0.0), axis=-1, keepdims=True)

    lo = EXPERT_LANE0 + grp * EXPERTS_PER_GROUP
    in_grp = (lane >= lo) & (lane < lo + EXPERTS_PER_GROUP)
    el = jnp.where(in_grp, logits, neg_inf)
    v0 = jnp.max(el, axis=-1, keepdims=True)
    j0 = first_lane(el == v0)
    el1 = jnp.where(lane == j0, neg_inf, el)
    v1 = jnp.max(el1, axis=-1, keepdims=True)
    j1 = first_lane(el1 == v1)
    e1 = jnp.exp(v1 - v0)
    gate0 = p_grp * (1.0 / (1.0 + e1))
    gate1 = p_grp * (e1 / (1.0 + e1))

    oh0 = lane == j0
    oh1 = lane == j1
    cnt = jnp.where(oh0 | oh1, 1.0, 0.0)
    r = lax.broadcasted_iota(I32, (tm, tm), 0)
    c = lax.broadcasted_iota(I32, (tm, tm), 1)
    strict_lower = jnp.where(c < r, 1.0, 0.0).astype(BF16)
    before = jnp.dot(strict_lower, cnt.astype(BF16), preferred_element_type=F32) + carry_ref[...]
    rank0 = jnp.sum(jnp.where(oh0, before, 0.0), axis=-1, keepdims=True).astype(I32)
    rank1 = jnp.sum(jnp.where(oh1, before, 0.0), axis=-1, keepdims=True).astype(I32)
    carry_ref[...] = carry_ref[...] + jnp.sum(cnt, axis=0, keepdims=True)
    cnt_ref[...] = carry_ref[...]

    meta_ref[...] = _pick_lanes(lane, j0 - EXPERT_LANE0, j1 - EXPERT_LANE0, rank0, rank1)
    gate_ref[...] = _pick_lanes(lane, gate0, gate1)


def _router(h, g, w_group, b_group, w_expert, b_expert):
    t, d = h.shape
    tm = _largest_tile(t, ROUTER_TILE, SUBLANES)
    pad = LANES - N_EXPERT_GROUPS - N_EXPERTS
    wr = jnp.concatenate([w_group, w_expert, jnp.zeros((d, pad), F32)], axis=1)
    w_hi = wr.astype(BF16)
    w_lo = (wr - w_hi.astype(F32)).astype(BF16)
    wr = jnp.concatenate([w_hi, w_lo], axis=1)
    br = jnp.concatenate([b_group, b_expert, jnp.zeros((pad,), F32)]).reshape(1, LANES)
    return pl.pallas_call(
        _router_kernel,
        out_shape=(jax.ShapeDtypeStruct((t, d // 2), U32),
                   jax.ShapeDtypeStruct((t, LANES), I32),
                   jax.ShapeDtypeStruct((t, LANES), F32),
                   jax.ShapeDtypeStruct((1, LANES), F32)),
        grid=(t // tm,),
        in_specs=[pl.BlockSpec((tm, d), lambda i: (i, 0)),
                  pl.BlockSpec((1, d), lambda i: (0, 0)),
                  pl.BlockSpec((d, 2 * LANES), lambda i: (0, 0)),
                  pl.BlockSpec((1, LANES), lambda i: (0, 0))],
        out_specs=(pl.BlockSpec((tm, d // 2), lambda i: (i, 0)),
                   pl.BlockSpec((tm, LANES), lambda i: (i, 0)),
                   pl.BlockSpec((tm, LANES), lambda i: (i, 0)),
                   pl.BlockSpec((1, LANES), lambda i: (0, 0))),
        scratch_shapes=[pltpu.VMEM((1, LANES), F32)],
        compiler_params=_params("arbitrary"),
        name="moe_router",
    )(h, g.reshape(1, d), wr, br)


DMA_PRIORITIES = 2


def _row_copy(src, src_row, dst, dst_row, sem):
    return pltpu.make_async_copy(src.at[pl.ds(src_row, 1)], dst.at[pl.ds(dst_row, 1)], sem)


def _block_rows(ref, blk):
    return ref.at[pl.ds(pl.multiple_of(blk * MOE_BLOCK, MOE_BLOCK), MOE_BLOCK)]


def _dispatch_kernel(zflag_ref, xf_ref, meta_ref, pstart_ref, xs_ref, dest_ref,
                     dest_vmem, dest_smem, zbuf, sem, row_sem, zero_sem):
    tm = xf_ref.shape[0]

    @pl.when(pl.program_id(0) == 0)
    def _():
        zbuf[...] = jnp.zeros_like(zbuf)

        def zero_block(blk, carry, wait):
            @pl.when(zflag_ref[blk] != 0)
            def _():
                cp = pltpu.make_async_copy(zbuf, _block_rows(xs_ref, blk), zero_sem)
                if wait:
                    cp.wait()
                else:
                    cp.start()
            return carry

        n_blocks = zflag_ref.shape[0]
        lax.fori_loop(0, n_blocks, functools.partial(zero_block, wait=False), 0)
        lax.fori_loop(0, n_blocks, functools.partial(zero_block, wait=True), 0)

    lane = lax.broadcasted_iota(I32, (tm, LANES), 1)
    meta = meta_ref[...]
    pstart = pstart_ref[...]
    dests = []
    for k in range(TOP_K):
        e = meta[:, k:k + 1]
        rank = meta[:, TOP_K + k:TOP_K + k + 1]
        base = jnp.sum(jnp.where(lane - EXPERT_LANE0 == e, pstart, 0.0), axis=-1, keepdims=True)
        dests.append(base + rank.astype(F32))
    dest_t = jnp.transpose(_pick_lanes(lane, *dests))[0:SUBLANES, :].astype(I32)
    dest_ref[0] = dest_t
    dest_vmem[...] = dest_t
    cp = pltpu.make_async_copy(dest_vmem, dest_smem, sem)
    cp.start()
    cp.wait()

    def issue(tok, carry):
        for k in range(TOP_K):
            _row_copy(xf_ref, tok, xs_ref, dest_smem[k, tok], row_sem).start(priority=k % DMA_PRIORITIES)
        return carry

    lax.fori_loop(0, tm, issue, 0)

    def drain(tok, carry):
        for k in range(TOP_K):
            _row_copy(xf_ref, 0, xs_ref, 0, row_sem).wait()
        return carry

    lax.fori_loop(0, tm, drain, 0)


def _dispatch(xf, meta, pstart_lanes, zero_flags):
    t, d = xf.shape
    tm = CHUNK
    nt = t // tm
    n_rows = zero_flags.shape[0] * MOE_BLOCK
    grid_spec = pltpu.PrefetchScalarGridSpec(
        num_scalar_prefetch=1,
        grid=(nt,),
        in_specs=[pl.BlockSpec((tm, d), lambda i, zf: (i, 0)),
                  pl.BlockSpec((tm, LANES), lambda i, zf: (i, 0)),
                  pl.BlockSpec((1, LANES), lambda i, zf: (0, 0))],
        out_specs=(pl.BlockSpec(memory_space=pl.ANY),
                   pl.BlockSpec((1, SUBLANES, LANES), lambda i, zf: (i, 0, 0))),
        scratch_shapes=[pltpu.VMEM((SUBLANES, LANES), I32),
                        pltpu.SMEM((SUBLANES, LANES), I32),
                        pltpu.VMEM((MOE_BLOCK, d), xf.dtype),
                        pltpu.SemaphoreType.DMA,
                        pltpu.SemaphoreType.DMA,
                        pltpu.SemaphoreType.DMA])
    return pl.pallas_call(
        _dispatch_kernel,
        out_shape=(jax.ShapeDtypeStruct((n_rows, d), xf.dtype),
                   jax.ShapeDtypeStruct((nt, SUBLANES, LANES), I32)),
        grid_spec=grid_spec,
        compiler_params=_params("arbitrary"),
        name="moe_dispatch",
    )(zero_flags, xf, meta, pstart_lanes)


FFN_DOWN_COLS = 1024


def _ffn_kernel(pe_ref, pb_ref, pn_ref, pz_ref, xs_ref, wg_ref, wu_ref, wd_ref, ys_ref,
                xbuf, acc, ypk, zbuf, in_sem, out_sem, zero_sem, *, n_ftiles, n_pass):
    del pe_ref
    p = pl.program_id(0)
    j = pl.program_id(1)
    d = acc.shape[1]
    half = d // 2
    last_tile = j == n_ftiles - 1
    nblk = pn_ref[p]
    slot = lax.rem(p, 2)
    p_next = jnp.minimum(p + 1, n_pass - 1)
    nblk_next = jnp.where(p + 1 < n_pass, pn_ref[p_next], 0)
    p_prev = jnp.maximum(p - 1, 0)
    nblk_prev = jnp.where(p > 0, pn_ref[p_prev], 0)

    def for_blocks(count, fn):
        for s in range(PASS_BLOCKS):
            pl.when(s < count)(functools.partial(fn, s))

    def block_slice(s):
        return pl.ds(s * MOE_BLOCK, MOE_BLOCK)

    def x_copy(q, s):
        q_slot = lax.rem(q, 2)
        return pltpu.make_async_copy(_block_rows(xs_ref, pb_ref[q] + s),
                                     xbuf.at[q_slot, block_slice(s)], in_sem.at[q_slot])

    def y_copy(q, s):
        return pltpu.make_async_copy(ypk.at[block_slice(s)], _block_rows(ys_ref, pb_ref[q] + s), out_sem)

    def wait_prev_writeback():
        for_blocks(nblk_prev, lambda s: y_copy(p_prev, s).wait())

    @pl.when((p == 0) & (j == 0))
    def _():
        zbuf[...] = jnp.zeros_like(zbuf)
        acc[...] = jnp.zeros_like(acc)
        for_blocks(nblk, lambda s: x_copy(p, s).start())

    @pl.when(j == 0)
    def _():
        def zero_copy(s):
            return pltpu.make_async_copy(zbuf, _block_rows(ys_ref, pb_ref[p] + s), zero_sem)

        for_blocks(pz_ref[p], lambda s: zero_copy(s).start())
        for_blocks(pz_ref[p], lambda s: zero_copy(s).wait())
        for_blocks(nblk, lambda s: x_copy(p, s).wait())
        for_blocks(nblk_next, lambda s: x_copy(p_next, s).start())

    def compute(n):
        rows = n * MOE_BLOCK
        x_lo, x_hi = _unpack_halves(xbuf[slot, 0:rows, :])
        x_lo = x_lo.astype(BF16)
        x_hi = x_hi.astype(BF16)

        def in_proj(w_ref):
            return (jnp.dot(x_lo, w_ref[0:half, :].astype(BF16), preferred_element_type=F32)
                    + jnp.dot(x_hi, w_ref[half:d, :].astype(BF16), preferred_element_type=F32))

        gate = in_proj(wg_ref)
        up = in_proj(wu_ref)
        hmid = (gate * _sigmoid(gate) * up).astype(BF16)
        down_cols = min(FFN_DOWN_COLS, d)
        for c0 in range(0, d, down_cols):
            cols = slice(c0, c0 + down_cols)
            y = jnp.dot(hmid, wd_ref[:, cols].astype(BF16), preferred_element_type=F32)
            acc[0:rows, cols] = y + jnp.where(j == 0, 0.0, acc[0:rows, cols])

        @pl.when(last_tile)
        def _():
            wait_prev_writeback()
            ypk[0:rows, :] = _pack_halves(acc[0:rows, :])

    for n in range(1, PASS_BLOCKS + 1):
        pl.when(nblk == n)(functools.partial(compute, n))

    pl.when(last_tile & (nblk == 0))(wait_prev_writeback)

    @pl.when(last_tile)
    def _():
        for_blocks(nblk, lambda s: y_copy(p, s).start())

        @pl.when(p == n_pass - 1)
        def _():
            for_blocks(nblk, lambda s: y_copy(p, s).wait())


def _ffn(xs, w_gate, w_up, w_down, layer, pass_expert, pass_block, pass_nblk, pass_nzero):
    n_rows, half = xs.shape
    d = 2 * half
    de = w_gate.shape[-1]
    assert d % min(FFN_DOWN_COLS, d) == 0 and w_gate.shape[-2] == d
    tf = _largest_tile(de, 256, LANES)
    n_ftiles = de // tf
    n_pass = pass_expert.shape[0]
    last = n_ftiles - 1

    def f_idx(p, j, pn):
        return jnp.where(pn[p] > 0, j, last)

    def in_proj(p, j, pe, pb, pn, pz):
        return (layer, pe[p], 0, f_idx(p, j, pn))

    def out_proj(p, j, pe, pb, pn, pz):
        return (layer, pe[p], f_idx(p, j, pn), 0)

    grid_spec = pltpu.PrefetchScalarGridSpec(
        num_scalar_prefetch=4,
        grid=(n_pass, n_ftiles),
        in_specs=[pl.BlockSpec(memory_space=pl.ANY),
                  pl.BlockSpec((None, None, d, tf), in_proj),
                  pl.BlockSpec((None, None, d, tf), in_proj),
                  pl.BlockSpec((None, None, tf, d), out_proj)],
        out_specs=pl.BlockSpec(memory_space=pl.ANY),
        scratch_shapes=[pltpu.VMEM((2, PASS_BLOCKS * MOE_BLOCK, half), U32),
                        pltpu.VMEM((PASS_BLOCKS * MOE_BLOCK, d), F32),
                        pltpu.VMEM((PASS_BLOCKS * MOE_BLOCK, half), U32),
                        pltpu.VMEM((MOE_BLOCK, half), U32),
                        pltpu.SemaphoreType.DMA((2,)),
                        pltpu.SemaphoreType.DMA,
                        pltpu.SemaphoreType.DMA])
    return pl.pallas_call(
        functools.partial(_ffn_kernel, n_ftiles=n_ftiles, n_pass=n_pass),
        out_shape=jax.ShapeDtypeStruct((n_rows, half), U32),
        grid_spec=grid_spec,
        compiler_params=_params("arbitrary", "arbitrary"),
        name="moe_ffn",
    )(pass_expert, pass_block, pass_nblk, pass_nzero, xs, w_gate, w_up, w_down)


def _combine_kernel(h_ref, gate_ref, dest_ref, g_ref, ys_ref, out_a, out_b, ybuf, dest_smem, sem, row_sem,
                    *, final, n_prompt_tiles):
    tm = h_ref.shape[0]
    cp = pltpu.make_async_copy(dest_ref.at[0], dest_smem, sem)
    cp.start()
    cp.wait()

    def issue(tok, carry):
        for k in range(TOP_K):
            _row_copy(ys_ref, dest_smem[k, tok], ybuf.at[k], tok, row_sem).start(priority=k % DMA_PRIORITIES)
        return carry

    lax.fori_loop(0, tm, issue, 0)

    def drain(tok, carry):
        for k in range(TOP_K):
            _row_copy(ys_ref, 0, ybuf.at[k], 0, row_sem).wait()
        return carry

    lax.fori_loop(0, tm, drain, 0)

    gates = gate_ref[...]
    f_lo, f_hi = None, None
    for k in range(TOP_K):
        y_lo, y_hi = _unpack_halves(ybuf[k])
        gate = gates[:, k:k + 1]
        f_lo = y_lo * gate if f_lo is None else f_lo + y_lo * gate
        f_hi = y_hi * gate if f_hi is None else f_hi + y_hi * gate
    hn = h_ref[...] + jnp.concatenate([f_lo, f_hi], axis=1)
    normed = hn * lax.rsqrt(jnp.mean(hn * hn, axis=-1, keepdims=True) + EPS) * g_ref[...]
    if final:
        i = pl.program_id(0)

        @pl.when(i < n_prompt_tiles)
        def _():
            out_a[...] = normed

        @pl.when(i >= n_prompt_tiles)
        def _():
            out_b[...] = normed
    else:
        out_a[...] = hn
        out_b[...] = normed.astype(out_b.dtype)


def _combine(h, gates, dest, ys, next_g, final, n_prompt_rows):
    t, d = h.shape
    tm = CHUNK
    assert n_prompt_rows % tm == 0 and t % tm == 0
    npt = n_prompt_rows // tm
    if final:
        out_shape = (jax.ShapeDtypeStruct((n_prompt_rows, d), F32),
                     jax.ShapeDtypeStruct((t - n_prompt_rows, d), F32))
        out_specs = (pl.BlockSpec((tm, d), lambda i: (jnp.minimum(i, npt - 1), 0)),
                     pl.BlockSpec((tm, d), lambda i: (jnp.maximum(i - npt, 0), 0)))
    else:
        out_shape = (jax.ShapeDtypeStruct((t, d), F32), jax.ShapeDtypeStruct((t, d), BF16))
        out_specs = (pl.BlockSpec((tm, d), lambda i: (i, 0)), pl.BlockSpec((tm, d), lambda i: (i, 0)))
    return pl.pallas_call(
        functools.partial(_combine_kernel, final=final, n_prompt_tiles=npt),
        out_shape=out_shape,
        grid=(t // tm,),
        in_specs=[pl.BlockSpec((tm, d), lambda i: (i, 0)),
                  pl.BlockSpec((tm, LANES), lambda i: (i, 0)),
                  pl.BlockSpec((1, SUBLANES, LANES), lambda i: (i, 0, 0)),
                  pl.BlockSpec((1, d), lambda i: (0, 0)),
                  pl.BlockSpec(memory_space=pl.ANY)],
        out_specs=out_specs,
        scratch_shapes=[pltpu.VMEM((TOP_K, tm, d // 2), U32),
                        pltpu.SMEM((SUBLANES, LANES), I32),
                        pltpu.SemaphoreType.DMA,
                        pltpu.SemaphoreType.DMA],
        compiler_params=_params("arbitrary"),
        name="moe_combine",
    )(h, gates, dest, next_g.reshape(1, d), ys)


def _pass_tables(counts, n_blocks, n_pass):
    blocks = (counts + MOE_BLOCK - 1) // MOE_BLOCK
    block_end = jnp.cumsum(blocks)
    block_start = block_end - blocks
    used = block_end[-1]
    passes = (blocks + PASS_BLOCKS - 1) // PASS_BLOCKS
    pass_end = jnp.cumsum(passes)
    pass_start = pass_end - passes
    total = pass_end[-1]
    p = jnp.arange(n_pass, dtype=I32)
    valid = p < total
    pc = jnp.where(valid, p, jnp.maximum(total - 1, 0))
    e = jnp.minimum(jnp.searchsorted(pass_end, pc, side="right"), N_EXPERTS - 1).astype(I32)
    k = pc - pass_start[e]
    tail_first = used + (p - total) * PASS_BLOCKS
    first = jnp.where(valid, block_start[e] + k * PASS_BLOCKS, tail_first)
    nblk = jnp.where(valid, jnp.clip(blocks[e] - k * PASS_BLOCKS, 0, PASS_BLOCKS), 0)
    nzero = jnp.where(valid, 0, jnp.clip(n_blocks - tail_first, 0, PASS_BLOCKS))
    blk = jnp.arange(n_blocks, dtype=I32)
    ragged_last = jnp.where(counts % MOE_BLOCK != 0, block_end - 1, n_blocks)
    zero_flags = (blk >= used) | jnp.zeros((n_blocks,), bool).at[ragged_last].set(True, mode="drop")
    return (e, first.astype(I32), nblk.astype(I32), nzero.astype(I32),
            (block_start * MOE_BLOCK).astype(I32), zero_flags.astype(I32))


def _hier_moe(h, norm_g, w_group, b_group, w_expert, b_expert, w_gate, w_up, w_down, layer,
              next_g, final, n_prompt_rows):
    t, d = h.shape
    n_assign = t * TOP_K
    n_blocks = -(-(n_assign + N_EXPERTS * (MOE_BLOCK - 1)) // MOE_BLOCK)
    n_pass = (n_blocks + N_EXPERTS * (PASS_BLOCKS - 1)) // PASS_BLOCKS + 2
    xf, meta, gates, cnt = _router(h, norm_g, w_group, b_group, w_expert, b_expert)
    counts = cnt[0, EXPERT_LANE0:EXPERT_LANE0 + N_EXPERTS].astype(I32)
    pass_expert, pass_block, pass_nblk, pass_nzero, row_start, zero_flags = _pass_tables(
        counts, n_blocks, n_pass)
    pstart_lanes = jnp.zeros((1, LANES), F32).at[0, EXPERT_LANE0:EXPERT_LANE0 + N_EXPERTS].set(
        row_start.astype(F32))
    xs, dest = _dispatch(xf, meta, pstart_lanes, zero_flags)
    ys = _ffn(xs, w_gate, w_up, w_down, layer, pass_expert, pass_block, pass_nblk, pass_nzero)
    return _combine(h, gates, dest, ys, next_g, final, n_prompt_rows)


def _embed_kernel(xp_ref, xs_ref, g_ref, h_ref, n_ref, *, n_prompt_tiles):
    i = pl.program_id(0)

    def emit(x):
        h_ref[...] = x
        y = x * lax.rsqrt(jnp.mean(x * x, axis=-1, keepdims=True) + EPS)
        n_ref[...] = (y * g_ref[...]).astype(n_ref.dtype)

    @pl.when(i < n_prompt_tiles)
    def _():
        emit(xp_ref[...])

    @pl.when(i >= n_prompt_tiles)
    def _():
        emit(xs_ref[...])


def _embed(x_prompt, x_sample, g):
    n_p, d = x_prompt.shape
    n_s = x_sample.shape[0]
    tm = CHUNK
    assert n_p % tm == 0 and n_s % tm == 0
    npt = n_p // tm
    t = n_p + n_s
    return pl.pallas_call(
        functools.partial(_embed_kernel, n_prompt_tiles=npt),
        out_shape=(jax.ShapeDtypeStruct((t, d), F32), jax.ShapeDtypeStruct((t, d), BF16)),
        grid=(t // tm,),
        in_specs=[pl.BlockSpec((tm, d), lambda i: (jnp.minimum(i, npt - 1), 0)),
                  pl.BlockSpec((tm, d), lambda i: (jnp.maximum(i - npt, 0), 0)),
                  pl.BlockSpec((1, d), lambda i: (0, 0))],
        out_specs=(pl.BlockSpec((tm, d), lambda i: (i, 0)), pl.BlockSpec((tm, d), lambda i: (i, 0))),
        compiler_params=_params("arbitrary"),
        name="embed_norm",
    )(x_prompt, x_sample, g.reshape(1, d))


def kernel(x_prompt, x_sample, state_conv, mix_norm_g, sgu_w_in, sgu_b_in, sgu_v_norm_g, sgu_v_norm_b, sgu_w_s, sgu_b_s, sgu_w_out, sgu_b_out, conv_w_pw1, conv_b_pw1, conv_w_dw, conv_b_dw, conv_ln_g, conv_ln_b, conv_w_pw2, conv_b_pw2, ffn_norm_g, moe_w_group, moe_b_group, moe_w_expert, moe_b_expert, moe_w_gate, moe_w_up, moe_w_down, final_norm_g):
    batch, seq, d = x_prompt.shape
    dec_batch, dec_seq, _ = x_sample.shape
    assert dec_seq == 1
    n_p = batch * seq
    depth = mix_norm_g.shape[0]
    h, hn = _embed(x_prompt.reshape(n_p, d), x_sample.reshape(dec_batch, d), mix_norm_g[0])

    sgu_v_new, conv_p_new, conv_s_new = [], [], []
    for i in range(depth):
        j = i // 2
        final = i == depth - 1
        if i % 2 == 0:
            z = _mm_gelu(hn, sgu_w_in[j], sgu_b_in[j])
            us, v_s = _sgu_mid(z, sgu_v_norm_g[j], sgu_v_norm_b[j], sgu_w_s[j], sgu_b_s[j], n_p)
            h = _mm_res(us, sgu_w_out[j], sgu_b_out[j], h)
            sgu_v_new.append(v_s.reshape(dec_batch, dec_seq, -1))
        else:
            glu = _mm_glu(hn, conv_w_pw1[j], conv_b_pw1[j])
            conv_args = (conv_w_dw[j], conv_b_dw[j], conv_ln_g[j], conv_ln_b[j])
            y_s, c_s = _conv_sample(state_conv[j], glu, *conv_args, n_p)
            yc = _conv_prompt(glu, y_s, *conv_args, n_p, seq)
            h = _mm_res(yc, conv_w_pw2[j], conv_b_pw2[j], h)
            tail = CONV_WIDTH - 1
            conv_p_new.append(jnp.stack([glu[(b + 1) * seq - tail:(b + 1) * seq] for b in range(batch)]))
            conv_s_new.append(c_s)
        next_g = final_norm_g if final else mix_norm_g[i + 1]
        h, hn = _hier_moe(h, ffn_norm_g[i], moe_w_group[i], moe_b_group[i], moe_w_expert[i],
                          moe_b_expert[i], moe_w_gate, moe_w_up, moe_w_down, i, next_g, final, n_p)

    y_prompt = h.reshape(batch, seq, d)
    y_sample = hn.reshape(dec_batch, dec_seq, d)
    def stack(parts):
        return parts[0][None] if len(parts) == 1 else jnp.stack(parts)

    return (y_prompt, y_sample, stack(sgu_v_new), stack(conv_p_new), stack(conv_s_new))
```

```python
import functools

import jax
import jax.numpy as jnp
from jax import lax
from jax.experimental import pallas as pl
from jax.experimental.pallas import tpu as pltpu

F32 = jnp.float32
BF16 = jnp.bfloat16
I32 = jnp.int32

EPS = 1e-6
CHUNK = 128
LANES = 128
SUBLANES = 8
CONV_WIDTH = 31
N_EXPERT_GROUPS = 8
EXPERTS_PER_GROUP = 8
N_EXPERTS = N_EXPERT_GROUPS * EXPERTS_PER_GROUP
TOP_K = 2
MOE_BLOCK = 128
PASS_BLOCKS = 3
EXPERT_LANE0 = N_EXPERT_GROUPS
VMEM_LIMIT_BYTES = 56 * 1024 * 1024


def _params(*sem):
    return pltpu.CompilerParams(dimension_semantics=sem, vmem_limit_bytes=VMEM_LIMIT_BYTES)


def _largest_tile(n, cap, mult):
    t = (min(cap, n) // mult) * mult
    while t > mult and n % t:
        t -= mult
    assert t >= mult and n % t == 0, (n, cap, mult)
    return t


def _sigmoid(x):
    return 1.0 / (1.0 + jnp.exp(-x))


def _stage_bf16(w_ref, wb_ref):
    @pl.when(pl.program_id(1) == 0)
    def _():
        wb_ref[...] = w_ref[...].astype(BF16)


def _mm_gelu_kernel(x_ref, w_ref, b_ref, o_ref, wb_ref):
    _stage_bf16(w_ref, wb_ref)
    acc = jnp.dot(x_ref[...], wb_ref[...], preferred_element_type=F32) + b_ref[...]
    o_ref[...] = jax.nn.gelu(acc).astype(o_ref.dtype)


def _mm_glu_kernel(x_ref, wa_ref, wg_ref, ba_ref, bg_ref, o_ref, wab_ref, wgb_ref):
    _stage_bf16(wa_ref, wab_ref)
    _stage_bf16(wg_ref, wgb_ref)
    x = x_ref[...]
    a = jnp.dot(x, wab_ref[...], preferred_element_type=F32) + ba_ref[...]
    gate = jnp.dot(x, wgb_ref[...], preferred_element_type=F32) + bg_ref[...]
    o_ref[...] = (a * _sigmoid(gate)).astype(o_ref.dtype)


def _mm_res_kernel(x_ref, w_ref, b_ref, r_ref, o_ref, wb_ref):
    _stage_bf16(w_ref, wb_ref)
    acc = jnp.dot(x_ref[...], wb_ref[...], preferred_element_type=F32) + b_ref[...]
    o_ref[...] = (r_ref[...] + acc).astype(o_ref.dtype)


MM_ROW_TILE = 832
MM_COL_TILE = 512


def _mm_call(kern, name, x, weights, col_blocks, biases, res, tn, n_out):
    t, k = x.shape
    tm = _largest_tile(t, MM_ROW_TILE, 16)
    in_specs = [pl.BlockSpec((tm, k), lambda j, i: (i, 0))]
    in_specs += [pl.BlockSpec((k, tn), functools.partial(lambda j, i, cb: (0, cb(j)), cb=cb)) for cb in col_blocks]
    in_specs += [pl.BlockSpec((1, tn), functools.partial(lambda j, i, cb: (0, cb(j)), cb=cb)) for cb in col_blocks]
    args = [x, *weights, *biases]
    if res is not None:
        in_specs.append(pl.BlockSpec((tm, tn), lambda j, i: (i, j)))
        args.append(res)
    return pl.pallas_call(
        kern,
        out_shape=jax.ShapeDtypeStruct((t, n_out), F32),
        grid=(n_out // tn, t // tm),
        in_specs=in_specs,
        out_specs=pl.BlockSpec((tm, tn), lambda j, i: (i, j)),
        scratch_shapes=[pltpu.VMEM((k, tn), BF16) for _ in weights],
        compiler_params=_params("arbitrary", "arbitrary"),
        name=name,
    )(*args)


def _mm_gelu(x, w, b):
    n = w.shape[1]
    tn = _largest_tile(n, MM_COL_TILE, LANES)
    return _mm_call(_mm_gelu_kernel, "mm_gelu", x, [w], [lambda j: j], [b.reshape(1, n)], None, tn, n)


def _mm_glu(x, w, b):
    n = w.shape[1] // 2
    tn = _largest_tile(n, MM_COL_TILE // 2, LANES)
    nb = n // tn
    b2 = b.reshape(1, 2 * n)
    return _mm_call(_mm_glu_kernel, "mm_glu", x, [w, w], [lambda j: j, lambda j: j + nb], [b2, b2], None, tn, n)


def _mm_res(x, w, b, res):
    n = w.shape[1]
    tn = _largest_tile(n, MM_COL_TILE, LANES)
    return _mm_call(_mm_res_kernel, "mm_res", x, [w], [lambda j: j], [b.reshape(1, n)], res, tn, n)


def _layer_norm(x, g, b):
    xc = x - jnp.mean(x, axis=-1, keepdims=True)
    y = xc * lax.rsqrt(jnp.mean(xc * xc, axis=-1, keepdims=True) + EPS)
    return y * g + b


def _sgu_mid_kernel(u_ref, v_ref, g_ref, b_ref, ws_ref, bias_ref, w00_ref, b00_ref,
                    us_ref, vs_ref, wt_ref, *, n_groups, n_prompt_tiles):
    i = pl.program_id(0)

    @pl.when(i == 0)
    def _():
        r = lax.broadcasted_iota(I32, (CHUNK, CHUNK), 0)
        c = lax.broadcasted_iota(I32, (CHUNK, CHUNK), 1)
        tril = (c <= r).astype(F32)
        for g in range(n_groups):
            wt_ref[g] = (ws_ref[g] * tril).astype(BF16)

    vn = _layer_norm(v_ref[...], g_ref[...], b_ref[...])

    @pl.when(i < n_prompt_tiles)
    def _():
        vb = vn.astype(BF16)
        for g in range(n_groups):
            sl = slice(g * LANES, (g + 1) * LANES)
            s = jnp.dot(wt_ref[g], vb[:, sl], preferred_element_type=F32) + bias_ref[:, sl]
            us_ref[:, sl] = (u_ref[:, sl] * s).astype(us_ref.dtype)

    @pl.when(i >= n_prompt_tiles)
    def _():
        s = vn * w00_ref[...] + b00_ref[...]
        us_ref[...] = (u_ref[...] * s).astype(us_ref.dtype)
        vs_ref[...] = vn


def _sgu_mid(z, v_g, v_b, w_s, b_s, n_prompt_rows):
    t, w2 = z.shape
    w = w2 // 2
    n_groups = w // LANES
    assert w_s.shape == (n_groups, CHUNK, CHUNK) and t % CHUNK == 0 and n_prompt_rows % CHUNK == 0
    npt = n_prompt_rows // CHUNK
    nst = t // CHUNK - npt
    bias_full = jnp.repeat(b_s.T, LANES, axis=1)
    w00 = jnp.repeat(w_s[:, 0, 0], LANES).reshape(1, w)
    b00 = jnp.repeat(b_s[:, 0], LANES).reshape(1, w)
    kern = functools.partial(_sgu_mid_kernel, n_groups=n_groups, n_prompt_tiles=npt)
    return pl.pallas_call(
        kern,
        out_shape=(jax.ShapeDtypeStruct((t, w), BF16),
                   jax.ShapeDtypeStruct((nst * CHUNK, w), F32)),
        grid=(t // CHUNK,),
        in_specs=[pl.BlockSpec((CHUNK, w), lambda i: (i, 0)),
                  pl.BlockSpec((CHUNK, w), lambda i: (i, 1)),
                  pl.BlockSpec((1, w), lambda i: (0, 0)),
                  pl.BlockSpec((1, w), lambda i: (0, 0)),
                  pl.BlockSpec((n_groups, CHUNK, CHUNK), lambda i: (0, 0, 0)),
                  pl.BlockSpec((CHUNK, w), lambda i: (0, 0)),
                  pl.BlockSpec((1, w), lambda i: (0, 0)),
                  pl.BlockSpec((1, w), lambda i: (0, 0))],
        out_specs=(pl.BlockSpec((CHUNK, w), lambda i: (i, 0)),
                   pl.BlockSpec((CHUNK, w), lambda i: (jnp.maximum(i - npt, 0), 0))),
        scratch_shapes=[pltpu.VMEM((n_groups, CHUNK, CHUNK), BF16)],
        compiler_params=_params("arbitrary"),
        name="sgu_mid",
    )(z, z, v_g.reshape(1, w), v_b.reshape(1, w), w_s, bias_full, w00, b00)


CONV_HALO = 32
CONV_TILE = 128


def _conv_prompt_kernel(x_ref, halo_ref, ysamp_ref, w_ref, bdw_ref, g_ref, b_ref, o_ref,
                        ext_ref, sh_ref, y_ref, *, tiles_per_seq, n_prompt_tiles):
    i = pl.program_id(0)
    n_strips = ext_ref.shape[0]
    ext_rows = CONV_HALO + CONV_TILE
    sh_rows = sh_ref.shape[1]
    first = CONV_HALO - (CONV_WIDTH - 1)

    @pl.when(i >= n_prompt_tiles)
    def _():
        o_ref[...] = ysamp_ref[...]

    @pl.when(i < n_prompt_tiles)
    def _():
        @pl.when(i % tiles_per_seq == 0)
        def _():
            ext_ref[:, 0:CONV_HALO, :] = jnp.zeros((n_strips, CONV_HALO, LANES), F32)

        @pl.when(i % tiles_per_seq != 0)
        def _():
            for s in range(n_strips):
                ext_ref[s, 0:CONV_HALO, :] = halo_ref[:, s * LANES:(s + 1) * LANES]

        for s in range(n_strips):
            ext_ref[s, CONV_HALO:ext_rows, :] = x_ref[:, s * LANES:(s + 1) * LANES]

        def strip(s, carry):
            for r in range(1, SUBLANES):
                sh_ref[r - 1] = ext_ref[s, r:r + sh_rows, :]
            taps_w = [jnp.broadcast_to(w_ref[s, k:k + 1, :], (SUBLANES, LANES)) for k in range(CONV_WIDTH)]
            bias = jnp.broadcast_to(bdw_ref[s], (SUBLANES, LANES))
            for rb in range(CONV_TILE // SUBLANES):
                acc = bias
                for k in range(CONV_WIDTH):
                    m, r = divmod(first + k, SUBLANES)
                    rows = slice((rb + m) * SUBLANES, (rb + m + 1) * SUBLANES)
                    tap = ext_ref[s, rows, :] if r == 0 else sh_ref[r - 1, rows, :]
                    acc = acc + tap * taps_w[k]
                y_ref[s, rb * SUBLANES:(rb + 1) * SUBLANES, :] = acc
            return carry

        lax.fori_loop(0, n_strips, strip, 0)

        inv_c = 1.0 / (n_strips * LANES)
        total = jnp.zeros((CONV_TILE, 1), F32)
        for s in range(n_strips):
            total = total + jnp.sum(y_ref[s], axis=-1, keepdims=True)
        mean = total * inv_c
        sq = jnp.zeros((CONV_TILE, 1), F32)
        for s in range(n_strips):
            yc = y_ref[s] - mean
            sq = sq + jnp.sum(yc * yc, axis=-1, keepdims=True)
        rstd = lax.rsqrt(sq * inv_c + EPS)
        for s in range(n_strips):
            cols = slice(s * LANES, (s + 1) * LANES)
            y = (y_ref[s] - mean) * rstd * g_ref[:, cols] + b_ref[:, cols]
            o_ref[:, cols] = (y * _sigmoid(y)).astype(o_ref.dtype)


def _conv_prompt(glu, y_sample, w_dw, b_dw, ln_g, ln_b, n_prompt_rows, seq):
    t, c = glu.shape
    assert seq % CONV_TILE == 0 and n_prompt_rows % seq == 0 and t % CONV_TILE == 0
    assert y_sample.shape == (t - n_prompt_rows, c) and c % LANES == 0
    ratio = CONV_TILE // CONV_HALO
    npt = n_prompt_rows // CONV_TILE
    ns = c // LANES
    kern = functools.partial(_conv_prompt_kernel, tiles_per_seq=seq // CONV_TILE, n_prompt_tiles=npt)
    w_strips = w_dw.reshape(CONV_WIDTH, ns, LANES).transpose(1, 0, 2)

    def prompt_tile(i):
        return jnp.minimum(i, npt - 1)

    return pl.pallas_call(
        kern,
        out_shape=jax.ShapeDtypeStruct((t, c), BF16),
        grid=(t // CONV_TILE,),
        in_specs=[pl.BlockSpec((CONV_TILE, c), lambda i: (prompt_tile(i), 0)),
                  pl.BlockSpec((CONV_HALO, c), lambda i: (jnp.maximum(prompt_tile(i) * ratio - 1, 0), 0)),
                  pl.BlockSpec((CONV_TILE, c), lambda i: (jnp.maximum(i - npt, 0), 0)),
                  pl.BlockSpec((ns, CONV_WIDTH, LANES), lambda i: (0, 0, 0)),
                  pl.BlockSpec((ns, 1, LANES), lambda i: (0, 0, 0)),
                  pl.BlockSpec((1, c), lambda i: (0, 0)),
                  pl.BlockSpec((1, c), lambda i: (0, 0))],
        out_specs=pl.BlockSpec((CONV_TILE, c), lambda i: (i, 0)),
        scratch_shapes=[pltpu.VMEM((ns, CONV_HALO + CONV_TILE, LANES), F32),
                        pltpu.VMEM((SUBLANES - 1, CONV_HALO + CONV_TILE - SUBLANES, LANES), F32),
                        pltpu.VMEM((ns, CONV_TILE, LANES), F32)],
        compiler_params=_params("arbitrary"),
        name="conv_prompt",
    )(glu, glu, y_sample, w_strips, b_dw.reshape(ns, 1, LANES), ln_g.reshape(1, c), ln_b.reshape(1, c))


CONV_SAMPLE_TILE = 16


def _conv_sample_kernel(hist_ref, x_ref, w_ref, bdw_ref, g_ref, b_ref, yc_ref, ns_ref):
    nh = CONV_WIDTH - 1
    x = x_ref[...]
    acc = x * w_ref[nh:nh + 1, :] + bdw_ref[...]
    for k in range(nh):
        acc = acc + hist_ref[k] * w_ref[k:k + 1, :]
    y = _layer_norm(acc, g_ref[...], b_ref[...])
    yc_ref[...] = (y * _sigmoid(y)).astype(yc_ref.dtype)
    for k in range(nh - 1):
        ns_ref[k] = hist_ref[k + 1]
    ns_ref[nh - 1] = x


def _conv_sample(hist, glu, w_dw, b_dw, ln_g, ln_b, n_prompt_rows):
    t, c = glu.shape
    db, nh, _ = hist.shape
    tb = CONV_SAMPLE_TILE
    assert nh == CONV_WIDTH - 1 and db % tb == 0 and n_prompt_rows % tb == 0 and t == n_prompt_rows + db
    b0 = n_prompt_rows // tb
    hist_t = jnp.transpose(hist, (1, 0, 2))
    yc, ns_t = pl.pallas_call(
        _conv_sample_kernel,
        out_shape=(jax.ShapeDtypeStruct((db, c), BF16),
                   jax.ShapeDtypeStruct((nh, db, c), F32)),
        grid=(db // tb,),
        in_specs=[pl.BlockSpec((nh, tb, c), lambda i: (0, i, 0)),
                  pl.BlockSpec((tb, c), lambda i: (i + b0, 0)),
                  pl.BlockSpec((CONV_WIDTH, c), lambda i: (0, 0)),
                  pl.BlockSpec((1, c), lambda i: (0, 0)),
                  pl.BlockSpec((1, c), lambda i: (0, 0)),
                  pl.BlockSpec((1, c), lambda i: (0, 0))],
        out_specs=(pl.BlockSpec((tb, c), lambda i: (i, 0)),
                   pl.BlockSpec((nh, tb, c), lambda i: (0, i, 0))),
        compiler_params=_params("arbitrary"),
        name="conv_sample",
    )(hist_t, glu, w_dw, b_dw.reshape(1, c), ln_g.reshape(1, c), ln_b.reshape(1, c))
    return yc, jnp.transpose(ns_t, (1, 0, 2))


ROUTER_TILE = 320


U32 = jnp.uint32


def _pack_halves(x):
    half = x.shape[1] // 2

    def bf16_bits_high(v):
        u = pltpu.bitcast(v, U32)
        return (u + U32(0x7FFF) + ((u >> 16) & U32(1))) & U32(0xFFFF0000)

    return bf16_bits_high(x[:, half:]) | (bf16_bits_high(x[:, :half]) >> 16)


def _unpack_halves(w):
    return pltpu.bitcast(w << 16, F32), pltpu.bitcast(w & U32(0xFFFF0000), F32)


def _pick_lanes(lane, *cols):
    out = jnp.zeros((cols[0].shape[0], LANES), cols[0].dtype)
    for k, col in enumerate(cols):
        out = jnp.where(lane == k, col, out)
    return out


def _router_kernel(h_ref, g_ref, wr_ref, br_ref, xf_ref, meta_ref, gate_ref, cnt_ref, carry_ref):
    i = pl.program_id(0)
    tm = h_ref.shape[0]

    @pl.when(i == 0)
    def _():
        carry_ref[...] = jnp.zeros_like(carry_ref)

    x = h_ref[...]
    xf = x * lax.rsqrt(jnp.mean(x * x, axis=-1, keepdims=True) + EPS) * g_ref[...]
    xf_ref[...] = _pack_halves(xf)
    logits = jnp.dot(xf.astype(BF16), wr_ref[...], preferred_element_type=F32) + br_ref[...]
    lane = lax.broadcasted_iota(I32, (tm, LANES), 1)
    lane_f = lane.astype(F32)
    neg_inf = jnp.float32(-jnp.inf)

    def first_lane(hit):
        return jnp.min(jnp.where(hit, lane_f, float(LANES)), axis=-1, keepdims=True).astype(I32)

    is_g = lane < N_EXPERT_GROUPS
    gl = jnp.where(is_g, logits, neg_inf)
    gmax = jnp.max(gl, axis=-1, keepdims=True)
    grp = first_lane(gl == gmax)
    p_grp = 1.0 / jnp.sum(jnp.where(is_g, jnp.exp(logits - gmax), 0.0), axis=-1, keepdims=True)

    lo = EXPERT_LANE0 + grp * EXPERTS_PER_GROUP
    in_grp = (lane >= lo) & (lane < lo + EXPERTS_PER_GROUP)
    el = jnp.where(in_grp, logits, neg_inf)
    v0 = jnp.max(el, axis=-1, keepdims=True)
    j0 = first_lane(el == v0)
    el1 = jnp.where(lane == j0, neg_inf, el)
    v1 = jnp.max(el1, axis=-1, keepdims=True)
    j1 = first_lane(el1 == v1)
    e1 = jnp.exp(v1 - v0)
    gate0 = p_grp * (1.0 / (1.0 + e1))
    gate1 = p_grp * (e1 / (1.0 + e1))

    oh0 = lane == j0
    oh1 = lane == j1
    cnt = jnp.where(oh0 | oh1, 1.0, 0.0)
    r = lax.broadcasted_iota(I32, (tm, tm), 0)
    c = lax.broadcasted_iota(I32, (tm, tm), 1)
    strict_lower = jnp.where(c < r, 1.0, 0.0).astype(BF16)
    before = jnp.dot(strict_lower, cnt.astype(BF16), preferred_element_type=F32) + carry_ref[...]
    rank0 = jnp.sum(jnp.where(oh0, before, 0.0), axis=-1, keepdims=True).astype(I32)
    rank1 = jnp.sum(jnp.where(oh1, before, 0.0), axis=-1, keepdims=True).astype(I32)
    carry_ref[...] = carry_ref[...] + jnp.sum(cnt, axis=0, keepdims=True)
    cnt_ref[...] = carry_ref[...]

    meta_ref[...] = _pick_lanes(lane, j0 - EXPERT_LANE0, j1 - EXPERT_LANE0, rank0, rank1)
    gate_ref[...] = _pick_lanes(lane, gate0, gate1)


def _router(h, g, w_group, b_group, w_expert, b_expert):
    t, d = h.shape
    tm = _largest_tile(t, ROUTER_TILE, SUBLANES)
    pad = LANES - N_EXPERT_GROUPS - N_EXPERTS
    wr = jnp.concatenate([w_group, w_expert, jnp.zeros((d, pad), F32)], axis=1)
    wr = wr.astype(BF16)
    br = jnp.concatenate([b_group, b_expert, jnp.zeros((pad,), F32)]).reshape(1, LANES)
    return pl.pallas_call(
        _router_kernel,
        out_shape=(jax.ShapeDtypeStruct((t, d // 2), U32),
                   jax.ShapeDtypeStruct((t, LANES), I32),
                   jax.ShapeDtypeStruct((t, LANES), F32),
                   jax.ShapeDtypeStruct((1, LANES), F32)),
        grid=(t // tm,),
        in_specs=[pl.BlockSpec((tm, d), lambda i: (i, 0)),
                  pl.BlockSpec((1, d), lambda i: (0, 0)),
                  pl.BlockSpec((d, LANES), lambda i: (0, 0)),
                  pl.BlockSpec((1, LANES), lambda i: (0, 0))],
        out_specs=(pl.BlockSpec((tm, d // 2), lambda i: (i, 0)),
                   pl.BlockSpec((tm, LANES), lambda i: (i, 0)),
                   pl.BlockSpec((tm, LANES), lambda i: (i, 0)),
                   pl.BlockSpec((1, LANES), lambda i: (0, 0))),
        scratch_shapes=[pltpu.VMEM((1, LANES), F32)],
        compiler_params=_params("arbitrary"),
        name="moe_router",
    )(h, g.reshape(1, d), wr, br)


def _row_copy(src, src_row, dst, dst_row, sem):
    return pltpu.make_async_copy(src.at[pl.ds(src_row, 1)], dst.at[pl.ds(dst_row, 1)], sem)


def _block_rows(ref, blk):
    return ref.at[pl.ds(pl.multiple_of(blk * MOE_BLOCK, MOE_BLOCK), MOE_BLOCK)]


def _dispatch_kernel(zflag_ref, xf_ref, meta_ref, pstart_ref, xs_ref, dest_ref,
                     dest_vmem, dest_smem, zbuf, stage, sem, row_sem, zero_sem, *, n_tiles):
    tm = xf_ref.shape[0]
    i = pl.program_id(0)
    slot = lax.rem(i, 2)

    def drain(s):
        def body(tok, carry):
            for k in range(TOP_K):
                _row_copy(stage.at[s], 0, xs_ref, 0, row_sem.at[s]).wait()
            return carry

        lax.fori_loop(0, tm, body, 0)

    @pl.when(i == 0)
    def _():
        zbuf[...] = jnp.zeros_like(zbuf)

        def zero_block(blk, carry, wait):
            @pl.when(zflag_ref[blk] != 0)
            def _():
                cp = pltpu.make_async_copy(zbuf, _block_rows(xs_ref, blk), zero_sem)
                if wait:
                    cp.wait()
                else:
                    cp.start()
            return carry

        n_blocks = zflag_ref.shape[0]
        lax.fori_loop(0, n_blocks, functools.partial(zero_block, wait=False), 0)
        lax.fori_loop(0, n_blocks, functools.partial(zero_block, wait=True), 0)

    lane = lax.broadcasted_iota(I32, (tm, LANES), 1)
    meta = meta_ref[...]
    pstart = pstart_ref[...]
    dests = []
    for k in range(TOP_K):
        e = meta[:, k:k + 1]
        rank = meta[:, TOP_K + k:TOP_K + k + 1]
        base = jnp.sum(jnp.where(lane - EXPERT_LANE0 == e, pstart, 0.0), axis=-1, keepdims=True)
        dests.append(base + rank.astype(F32))
    dest_t = jnp.transpose(_pick_lanes(lane, *dests))[0:SUBLANES, :].astype(I32)
    dest_ref[0] = dest_t
    dest_vmem[...] = dest_t
    cp = pltpu.make_async_copy(dest_vmem, dest_smem, sem)
    cp.start()
    cp.wait()

    pl.when(i >= 2)(functools.partial(drain, slot))
    stage[slot] = xf_ref[...]

    def issue(tok, carry):
        for k in range(TOP_K):
            _row_copy(stage.at[slot], tok, xs_ref, dest_smem[k, tok], row_sem.at[slot]).start()
        return carry

    lax.fori_loop(0, tm, issue, 0)

    @pl.when(i == n_tiles - 1)
    def _():
        pl.when(i >= 1)(functools.partial(drain, 1 - slot))
        drain(slot)


def _dispatch(xf, meta, pstart_lanes, zero_flags):
    t, d = xf.shape
    tm = CHUNK
    nt = t // tm
    n_rows = zero_flags.shape[0] * MOE_BLOCK
    grid_spec = pltpu.PrefetchScalarGridSpec(
        num_scalar_prefetch=1,
        grid=(nt,),
        in_specs=[pl.BlockSpec((tm, d), lambda i, zf: (i, 0)),
                  pl.BlockSpec((tm, LANES), lambda i, zf: (i, 0)),
                  pl.BlockSpec((1, LANES), lambda i, zf: (0, 0))],
        out_specs=(pl.BlockSpec(memory_space=pl.ANY),
                   pl.BlockSpec((1, SUBLANES, LANES), lambda i, zf: (i, 0, 0))),
        scratch_shapes=[pltpu.VMEM((SUBLANES, LANES), I32),
                        pltpu.SMEM((SUBLANES, LANES), I32),
                        pltpu.VMEM((MOE_BLOCK, d), xf.dtype),
                        pltpu.VMEM((2, tm, d), xf.dtype),
                        pltpu.SemaphoreType.DMA,
                        pltpu.SemaphoreType.DMA((2,)),
                        pltpu.SemaphoreType.DMA])
    return pl.pallas_call(
        functools.partial(_dispatch_kernel, n_tiles=nt),
        out_shape=(jax.ShapeDtypeStruct((n_rows, d), xf.dtype),
                   jax.ShapeDtypeStruct((nt, SUBLANES, LANES), I32)),
        grid_spec=grid_spec,
        compiler_params=_params("arbitrary"),
        name="moe_dispatch",
    )(zero_flags, xf, meta, pstart_lanes)


FFN_DOWN_COLS = 1024


def _ffn_kernel(pe_ref, pb_ref, pn_ref, pz_ref, xs_ref, wg_ref, wu_ref, wd_ref, ys_ref,
                xbuf, acc, ypk, zbuf, in_sem, out_sem, zero_sem, *, n_ftiles, n_pass):
    del pe_ref
    p = pl.program_id(0)
    j = pl.program_id(1)
    d = acc.shape[1]
    half = d // 2
    last_tile = j == n_ftiles - 1
    nblk = pn_ref[p]
    slot = lax.rem(p, 2)
    p_next = jnp.minimum(p + 1, n_pass - 1)
    nblk_next = jnp.where(p + 1 < n_pass, pn_ref[p_next], 0)
    p_prev = jnp.maximum(p - 1, 0)
    nblk_prev = jnp.where(p > 0, pn_ref[p_prev], 0)

    def for_blocks(count, fn):
        for s in range(PASS_BLOCKS):
            pl.when(s < count)(functools.partial(fn, s))

    def block_slice(s):
        return pl.ds(s * MOE_BLOCK, MOE_BLOCK)

    def x_copy(q, s):
        q_slot = lax.rem(q, 2)
        return pltpu.make_async_copy(_block_rows(xs_ref, pb_ref[q] + s),
                                     xbuf.at[q_slot, block_slice(s)], in_sem.at[q_slot])

    def y_copy(q, s):
        return pltpu.make_async_copy(ypk.at[block_slice(s)], _block_rows(ys_ref, pb_ref[q] + s), out_sem)

    def wait_prev_writeback():
        for_blocks(nblk_prev, lambda s: y_copy(p_prev, s).wait())

    @pl.when((p == 0) & (j == 0))
    def _():
        zbuf[...] = jnp.zeros_like(zbuf)
        acc[...] = jnp.zeros_like(acc)
        for_blocks(nblk, lambda s: x_copy(p, s).start())

    @pl.when(j == 0)
    def _():
        def zero_copy(s):
            return pltpu.make_async_copy(zbuf, _block_rows(ys_ref, pb_ref[p] + s), zero_sem)

        for_blocks(pz_ref[p], lambda s: zero_copy(s).start())
        for_blocks(pz_ref[p], lambda s: zero_copy(s).wait())
        for_blocks(nblk, lambda s: x_copy(p, s).wait())
        for_blocks(nblk_next, lambda s: x_copy(p_next, s).start())

    def compute(n):
        rows = n * MOE_BLOCK
        x_lo, x_hi = _unpack_halves(xbuf[slot, 0:rows, :])
        x_lo = x_lo.astype(BF16)
        x_hi = x_hi.astype(BF16)

        def in_proj(w_ref):
            return (jnp.dot(x_lo, w_ref[0:half, :].astype(BF16), preferred_element_type=F32)
                    + jnp.dot(x_hi, w_ref[half:d, :].astype(BF16), preferred_element_type=F32))

        gate = in_proj(wg_ref)
        up = in_proj(wu_ref)
        hmid = (gate * _sigmoid(gate) * up).astype(BF16)
        down_cols = min(FFN_DOWN_COLS, d)
        for c0 in range(0, d, down_cols):
            cols = slice(c0, c0 + down_cols)
            y = jnp.dot(hmid, wd_ref[:, cols].astype(BF16), preferred_element_type=F32)
            acc[0:rows, cols] = y + jnp.where(j == 0, 0.0, acc[0:rows, cols])

        @pl.when(last_tile)
        def _():
            wait_prev_writeback()
            ypk[0:rows, :] = _pack_halves(acc[0:rows, :])

    for n in range(1, PASS_BLOCKS + 1):
        pl.when(nblk == n)(functools.partial(compute, n))

    pl.when(last_tile & (nblk == 0))(wait_prev_writeback)

    @pl.when(last_tile)
    def _():
        for_blocks(nblk, lambda s: y_copy(p, s).start())

        @pl.when(p == n_pass - 1)
        def _():
            for_blocks(nblk, lambda s: y_copy(p, s).wait())


def _ffn(xs, w_gate, w_up, w_down, layer, pass_expert, pass_block, pass_nblk, pass_nzero):
    n_rows, half = xs.shape
    d = 2 * half
    de = w_gate.shape[-1]
    assert d % min(FFN_DOWN_COLS, d) == 0 and w_gate.shape[-2] == d
    tf = _largest_tile(de, 256, LANES)
    n_ftiles = de // tf
    n_pass = pass_expert.shape[0]
    last = n_ftiles - 1

    def f_idx(p, j, pn):
        return jnp.where(pn[p] > 0, j, last)

    def in_proj(p, j, pe, pb, pn, pz):
        return (layer, pe[p], 0, f_idx(p, j, pn))

    def out_proj(p, j, pe, pb, pn, pz):
        return (layer, pe[p], f_idx(p, j, pn), 0)

    grid_spec = pltpu.PrefetchScalarGridSpec(
        num_scalar_prefetch=4,
        grid=(n_pass, n_ftiles),
        in_specs=[pl.BlockSpec(memory_space=pl.ANY),
                  pl.BlockSpec((None, None, d, tf), in_proj),
                  pl.BlockSpec((None, None, d, tf), in_proj),
                  pl.BlockSpec((None, None, tf, d), out_proj)],
        out_specs=pl.BlockSpec(memory_space=pl.ANY),
        scratch_shapes=[pltpu.VMEM((2, PASS_BLOCKS * MOE_BLOCK, half), U32),
                        pltpu.VMEM((PASS_BLOCKS * MOE_BLOCK, d), F32),
                        pltpu.VMEM((PASS_BLOCKS * MOE_BLOCK, half), U32),
                        pltpu.VMEM((MOE_BLOCK, half), U32),
                        pltpu.SemaphoreType.DMA((2,)),
                        pltpu.SemaphoreType.DMA,
                        pltpu.SemaphoreType.DMA])
    return pl.pallas_call(
        functools.partial(_ffn_kernel, n_ftiles=n_ftiles, n_pass=n_pass),
        out_shape=jax.ShapeDtypeStruct((n_rows, half), U32),
        grid_spec=grid_spec,
        compiler_params=_params("arbitrary", "arbitrary"),
        name="moe_ffn",
    )(pass_expert, pass_block, pass_nblk, pass_nzero, xs, w_gate, w_up, w_down)


def _combine_kernel(h_ref, gate_ref, dest_ref, dest_next_ref, g_ref, ys_ref, out_a, out_b,
                    ybuf, dest_smem, sem, row_sem, *, final, n_prompt_tiles, n_tiles):
    tm = h_ref.shape[0]
    i = pl.program_id(0)
    slot = lax.rem(i, 2)

    def gather(dref, s):
        cp = pltpu.make_async_copy(dref.at[0], dest_smem, sem)
        cp.start()
        cp.wait()

        def issue(tok, carry):
            for k in range(TOP_K):
                _row_copy(ys_ref, dest_smem[k, tok], ybuf.at[s, k], tok, row_sem.at[s]).start()
            return carry

        lax.fori_loop(0, tm, issue, 0)

    pl.when(i == 0)(functools.partial(gather, dest_ref, slot))
    pl.when(i + 1 < n_tiles)(functools.partial(gather, dest_next_ref, 1 - slot))

    def drain(tok, carry):
        for k in range(TOP_K):
            _row_copy(ys_ref, 0, ybuf.at[slot, k], 0, row_sem.at[slot]).wait()
        return carry

    lax.fori_loop(0, tm, drain, 0)

    gates = gate_ref[...]
    f_lo, f_hi = None, None
    for k in range(TOP_K):
        y_lo, y_hi = _unpack_halves(ybuf[slot, k])
        gate = gates[:, k:k + 1]
        f_lo = y_lo * gate if f_lo is None else f_lo + y_lo * gate
        f_hi = y_hi * gate if f_hi is None else f_hi + y_hi * gate
    hn = h_ref[...] + jnp.concatenate([f_lo, f_hi], axis=1)
    normed = hn * lax.rsqrt(jnp.mean(hn * hn, axis=-1, keepdims=True) + EPS) * g_ref[...]
    if final:
        i = pl.program_id(0)

        @pl.when(i < n_prompt_tiles)
        def _():
            out_a[...] = normed

        @pl.when(i >= n_prompt_tiles)
        def _():
            out_b[...] = normed
    else:
        out_a[...] = hn
        out_b[...] = normed.astype(out_b.dtype)


def _combine(h, gates, dest, ys, next_g, final, n_prompt_rows):
    t, d = h.shape
    tm = CHUNK
    assert n_prompt_rows % tm == 0 and t % tm == 0
    npt = n_prompt_rows // tm
    if final:
        out_shape = (jax.ShapeDtypeStruct((n_prompt_rows, d), F32),
                     jax.ShapeDtypeStruct((t - n_prompt_rows, d), F32))
        out_specs = (pl.BlockSpec((tm, d), lambda i: (jnp.minimum(i, npt - 1), 0)),
                     pl.BlockSpec((tm, d), lambda i: (jnp.maximum(i - npt, 0), 0)))
    else:
        out_shape = (jax.ShapeDtypeStruct((t, d), F32), jax.ShapeDtypeStruct((t, d), BF16))
        out_specs = (pl.BlockSpec((tm, d), lambda i: (i, 0)), pl.BlockSpec((tm, d), lambda i: (i, 0)))
    nt = t // tm
    return pl.pallas_call(
        functools.partial(_combine_kernel, final=final, n_prompt_tiles=npt, n_tiles=nt),
        out_shape=out_shape,
        grid=(nt,),
        in_specs=[pl.BlockSpec((tm, d), lambda i: (i, 0)),
                  pl.BlockSpec((tm, LANES), lambda i: (i, 0)),
                  pl.BlockSpec((1, SUBLANES, LANES), lambda i: (i, 0, 0)),
                  pl.BlockSpec((1, SUBLANES, LANES), lambda i: (jnp.minimum(i + 1, nt - 1), 0, 0)),
                  pl.BlockSpec((1, d), lambda i: (0, 0)),
                  pl.BlockSpec(memory_space=pl.ANY)],
        out_specs=out_specs,
        scratch_shapes=[pltpu.VMEM((2, TOP_K, tm, d // 2), U32),
                        pltpu.SMEM((SUBLANES, LANES), I32),
                        pltpu.SemaphoreType.DMA,
                        pltpu.SemaphoreType.DMA((2,))],
        compiler_params=_params("arbitrary"),
        name="moe_combine",
    )(h, gates, dest, dest, next_g.reshape(1, d), ys)


def _pass_tables(counts, n_blocks, n_pass):
    blocks = (counts + MOE_BLOCK - 1) // MOE_BLOCK
    block_end = jnp.cumsum(blocks)
    block_start = block_end - blocks
    used = block_end[-1]
    passes = (blocks + PASS_BLOCKS - 1) // PASS_BLOCKS
    pass_end = jnp.cumsum(passes)
    pass_start = pass_end - passes
    total = pass_end[-1]
    p = jnp.arange(n_pass, dtype=I32)
    valid = p < total
    pc = jnp.where(valid, p, jnp.maximum(total - 1, 0))
    e = jnp.minimum(jnp.searchsorted(pass_end, pc, side="right"), N_EXPERTS - 1).astype(I32)
    k = pc - pass_start[e]
    tail_first = used + (p - total) * PASS_BLOCKS
    first = jnp.where(valid, block_start[e] + k * PASS_BLOCKS, tail_first)
    nblk = jnp.where(valid, jnp.clip(blocks[e] - k * PASS_BLOCKS, 0, PASS_BLOCKS), 0)
    nzero = jnp.where(valid, 0, jnp.clip(n_blocks - tail_first, 0, PASS_BLOCKS))
    blk = jnp.arange(n_blocks, dtype=I32)
    ragged_last = jnp.where(counts % MOE_BLOCK != 0, block_end - 1, n_blocks)
    zero_flags = (blk >= used) | jnp.zeros((n_blocks,), bool).at[ragged_last].set(True, mode="drop")
    return (e, first.astype(I32), nblk.astype(I32), nzero.astype(I32),
            (block_start * MOE_BLOCK).astype(I32), zero_flags.astype(I32))


def _hier_moe(h, norm_g, w_group, b_group, w_expert, b_expert, w_gate, w_up, w_down, layer,
              next_g, final, n_prompt_rows):
    t, d = h.shape
    n_assign = t * TOP_K
    n_blocks = -(-(n_assign + N_EXPERTS * (MOE_BLOCK - 1)) // MOE_BLOCK)
    n_pass = (n_blocks + N_EXPERTS * (PASS_BLOCKS - 1)) // PASS_BLOCKS + 2
    xf, meta, gates, cnt = _router(h, norm_g, w_group, b_group, w_expert, b_expert)
    counts = cnt[0, EXPERT_LANE0:EXPERT_LANE0 + N_EXPERTS].astype(I32)
    pass_expert, pass_block, pass_nblk, pass_nzero, row_start, zero_flags = _pass_tables(
        counts, n_blocks, n_pass)
    pstart_lanes = jnp.zeros((1, LANES), F32).at[0, EXPERT_LANE0:EXPERT_LANE0 + N_EXPERTS].set(
        row_start.astype(F32))
    xs, dest = _dispatch(xf, meta, pstart_lanes, zero_flags)
    ys = _ffn(xs, w_gate, w_up, w_down, layer, pass_expert, pass_block, pass_nblk, pass_nzero)
    return _combine(h, gates, dest, ys, next_g, final, n_prompt_rows)


def _embed_kernel(xp_ref, xs_ref, g_ref, h_ref, n_ref, *, n_prompt_tiles):
    i = pl.program_id(0)

    def emit(x):
        h_ref[...] = x
        y = x * lax.rsqrt(jnp.mean(x * x, axis=-1, keepdims=True) + EPS)
        n_ref[...] = (y * g_ref[...]).astype(n_ref.dtype)

    @pl.when(i < n_prompt_tiles)
    def _():
        emit(xp_ref[...])

    @pl.when(i >= n_prompt_tiles)
    def _():
        emit(xs_ref[...])


def _embed(x_prompt, x_sample, g):
    n_p, d = x_prompt.shape
    n_s = x_sample.shape[0]
    tm = CHUNK
    assert n_p % tm == 0 and n_s % tm == 0
    npt = n_p // tm
    t = n_p + n_s
    return pl.pallas_call(
        functools.partial(_embed_kernel, n_prompt_tiles=npt),
        out_shape=(jax.ShapeDtypeStruct((t, d), F32), jax.ShapeDtypeStruct((t, d), BF16)),
        grid=(t // tm,),
        in_specs=[pl.BlockSpec((tm, d), lambda i: (jnp.minimum(i, npt - 1), 0)),
                  pl.BlockSpec((tm, d), lambda i: (jnp.maximum(i - npt, 0), 0)),
                  pl.BlockSpec((1, d), lambda i: (0, 0))],
        out_specs=(pl.BlockSpec((tm, d), lambda i: (i, 0)), pl.BlockSpec((tm, d), lambda i: (i, 0))),
        compiler_params=_params("arbitrary"),
        name="embed_norm",
    )(x_prompt, x_sample, g.reshape(1, d))


def kernel(x_prompt, x_sample, state_conv, mix_norm_g, sgu_w_in, sgu_b_in, sgu_v_norm_g, sgu_v_norm_b, sgu_w_s, sgu_b_s, sgu_w_out, sgu_b_out, conv_w_pw1, conv_b_pw1, conv_w_dw, conv_b_dw, conv_ln_g, conv_ln_b, conv_w_pw2, conv_b_pw2, ffn_norm_g, moe_w_group, moe_b_group, moe_w_expert, moe_b_expert, moe_w_gate, moe_w_up, moe_w_down, final_norm_g):
    batch, seq, d = x_prompt.shape
    dec_batch, dec_seq, _ = x_sample.shape
    assert dec_seq == 1
    n_p = batch * seq
    depth = mix_norm_g.shape[0]
    h, hn = _embed(x_prompt.reshape(n_p, d), x_sample.reshape(dec_batch, d), mix_norm_g[0])

    sgu_v_new, conv_p_new, conv_s_new = [], [], []
    for i in range(depth):
        j = i // 2
        final = i == depth - 1
        if i % 2 == 0:
            z = _mm_gelu(hn, sgu_w_in[j], sgu_b_in[j])
            us, v_s = _sgu_mid(z, sgu_v_norm_g[j], sgu_v_norm_b[j], sgu_w_s[j], sgu_b_s[j], n_p)
            h = _mm_res(us, sgu_w_out[j], sgu_b_out[j], h)
            sgu_v_new.append(v_s.reshape(dec_batch, dec_seq, -1))
        else:
            glu = _mm_glu(hn, conv_w_pw1[j], conv_b_pw1[j])
            conv_args = (conv_w_dw[j], conv_b_dw[j], conv_ln_g[j], conv_ln_b[j])
            y_s, c_s = _conv_sample(state_conv[j], glu, *conv_args, n_p)
            yc = _conv_prompt(glu, y_s, *conv_args, n_p, seq)
            h = _mm_res(yc, conv_w_pw2[j], conv_b_pw2[j], h)
            tail = CONV_WIDTH - 1
            conv_p_new.append(jnp.stack([glu[(b + 1) * seq - tail:(b + 1) * seq] for b in range(batch)]))
            conv_s_new.append(c_s)
        next_g = final_norm_g if final else mix_norm_g[i + 1]
        h, hn = _hier_moe(h, ffn_norm_g[i], moe_w_group[i], moe_b_group[i], moe_w_expert[i],
                          moe_b_expert[i], moe_w_gate, moe_w_up, moe_w_down, i, next_g, final, n_p)

    y_prompt = h.reshape(batch, seq, d)
    y_sample = hn.reshape(dec_batch, dec_seq, d)
    def stack(parts):
        return parts[0][None] if len(parts) == 1 else jnp.stack(parts)

    return (y_prompt, y_sample, stack(sgu_v_new), stack(conv_p_new), stack(conv_s_new))
```
